```python
import math
import jax, jax.numpy as jnp
from jax import lax
import numpy as np

D_MODEL = 1024
BATCH = 4
SEQ = 8192
DEPTH = 2

D_PLE = 256
N_EVEN = (DEPTH + 1) // 2
N_ODD = DEPTH // 2
NORM_EPS = 1e-6

SSD_HEADS = 16
SSD_HEAD_DIM = 64
SSD_WIDTH = SSD_HEADS * SSD_HEAD_DIM
SSD_GROUPS = 2
SSD_STATE = 128
SSD_CONV = 4
SSD_CHUNK = 128
SSD_CONV_CH = SSD_WIDTH + 2 * SSD_GROUPS * SSD_STATE

DIFF_HEADS = 8
DIFF_HEAD_DIM = 64
DIFF_V_DIM = 2 * DIFF_HEAD_DIM
DIFF_QK_WIDTH = DIFF_HEADS * 2 * DIFF_HEAD_DIM
DIFF_WIDTH = DIFF_HEADS * DIFF_V_DIM
Q_BLOCK = 128
ROPE_THETA = 10000.0

EVEN_SPLITS = [
    SSD_WIDTH,
    SSD_WIDTH + SSD_CONV_CH,
    SSD_WIDTH + SSD_CONV_CH + SSD_HEADS,
    SSD_WIDTH + SSD_CONV_CH + SSD_HEADS + DIFF_QK_WIDTH,
    SSD_WIDTH + SSD_CONV_CH + SSD_HEADS + 2 * DIFF_QK_WIDTH,
    SSD_WIDTH + SSD_CONV_CH + SSD_HEADS + 2 * DIFF_QK_WIDTH + DIFF_WIDTH,
]
EVEN_IN = SSD_WIDTH + SSD_CONV_CH + SSD_HEADS + 2 * DIFF_QK_WIDTH + 2 * DIFF_WIDTH
EVEN_MIX = SSD_WIDTH + DIFF_WIDTH

CONF_WIDTH = 2 * D_MODEL
CONF_KERNEL = 31
CONF_IN = 3 * CONF_WIDTH

kernel_name = "hybrid_ssd_diffattn_conformer_block"


def rmsnorm(x, w):
    xf = x.astype(jnp.float32)
    y = xf * lax.rsqrt(jnp.mean(xf * xf, axis=-1, keepdims=True) + NORM_EPS)
    return (y * w.astype(jnp.float32)).astype(x.dtype)


def group_rmsnorm(x, w, groups):
    shp = x.shape
    xf = x.astype(jnp.float32).reshape(shp[:-1] + (groups, shp[-1] // groups))
    y = xf * lax.rsqrt(jnp.mean(xf * xf, axis=-1, keepdims=True) + NORM_EPS)
    return (y.reshape(shp) * w.astype(jnp.float32)).astype(x.dtype)


def layernorm(x, w, b):
    xf = x.astype(jnp.float32)
    mu = jnp.mean(xf, axis=-1, keepdims=True)
    var = jnp.mean(jnp.square(xf - mu), axis=-1, keepdims=True)
    y = (xf - mu) * lax.rsqrt(var + NORM_EPS)
    return (y * w.astype(jnp.float32) + b.astype(jnp.float32)).astype(x.dtype)


def causal_dwconv(x, w, b):
    k = w.shape[0]
    y = lax.conv_general_dilated(
        x, w[:, None, :], window_strides=(1,), padding=[(k - 1, 0)],
        dimension_numbers=("NWC", "WIO", "NWC"), feature_group_count=x.shape[-1])
    return y + b


def rope_tables(seq, dim):
    pos = jnp.arange(seq, dtype=jnp.float32)
    inv = ROPE_THETA ** (-jnp.arange(0, dim, 2, dtype=jnp.float32) / dim)
    ang = pos[:, None] * inv[None, :]
    return jnp.cos(ang), jnp.sin(ang)


def apply_rope(x, cos, sin):
    c = cos[:, None, None, :].astype(x.dtype)
    s = sin[:, None, None, :].astype(x.dtype)
    x1, x2 = jnp.split(x, 2, axis=-1)
    return jnp.concatenate([x1 * c - x2 * s, x2 * c + x1 * s], axis=-1)


def segsum_exp(a):
    t = a.shape[-1]
    cs = jnp.cumsum(a, axis=-1)
    diff = cs[..., :, None] - cs[..., None, :]
    mask = jnp.tril(jnp.ones((t, t), dtype=bool))
    return jnp.where(mask, jnp.exp(jnp.where(mask, diff, 0.0)), 0.0)


def ssd_scan(x, dt, a_neg, bm, cm):
    b, s, h, p = x.shape
    g, n = bm.shape[2], bm.shape[3]
    r = h // g
    c = s // SSD_CHUNK
    l = SSD_CHUNK
    xd = (x * dt[..., None]).reshape(b, c, l, g, r, p)
    a = (a_neg * dt).reshape(b, c, l, g, r).transpose(0, 3, 4, 1, 2)
    bc = bm.reshape(b, c, l, g, n)
    cc = cm.reshape(b, c, l, g, n)
    a_cs = jnp.cumsum(a, axis=-1)
    lmat = segsum_exp(a)
    cb = jnp.einsum("bclgn,bcsgn->bgcls", cc, bc)
    y_diag = jnp.einsum("bgcls,bgrcls,bcsgrp->bclgrp", cb, lmat, xd)
    decay_states = jnp.exp(a_cs[..., -1:] - a_cs)
    states = jnp.einsum("bclgn,bgrcl,bclgrp->bcgrpn", bc, decay_states, xd)
    chunk_decay = jnp.exp(a_cs[..., -1]).transpose(3, 0, 1, 2)

    def step(carry, inp):
        dec, st = inp
        return carry * dec[..., None, None] + st, carry

    init = jnp.zeros((b, g, r, p, n), dtype=states.dtype)
    _, prev = lax.scan(step, init, (chunk_decay, states.transpose(1, 0, 2, 3, 4, 5)))
    y_off = jnp.einsum("bclgn,cbgrpn,bgrcl->bclgrp", cc, prev, jnp.exp(a_cs))
    return (y_diag + y_off).reshape(b, s, h, p)


def diff_attention(q, k, v, lam):
    b, s, hh, _, d = q.shape
    nb = s // Q_BLOCK
    scale = d ** -0.5
    qb = q.reshape(b, nb, Q_BLOCK, hh, 2, d).transpose(1, 0, 2, 3, 4, 5)
    kpos = jnp.arange(s)

    def block(args):
        qi, start = args
        sc = jnp.einsum("bqhcd,bkhcd->bhcqk", qi, k).astype(jnp.float32) * scale
        qpos = start + jnp.arange(Q_BLOCK)
        mask = kpos[None, :] <= qpos[:, None]
        sc = jnp.where(mask, sc, -jnp.inf)
        att = jax.nn.softmax(sc, axis=-1)
        wgt = (att[:, :, 0] - lam * att[:, :, 1]).astype(v.dtype)
        return jnp.einsum("bhqk,bkhe->bqhe", wgt, v)

    out = lax.map(block, (qb, jnp.arange(nb) * Q_BLOCK))
    return out.transpose(1, 0, 2, 3, 4).reshape(b, s, hh, 2 * d)


def ssd_diff_mixer(hn, w_in, conv_w, conv_b, dt_bias, a_log, d_skip, ssd_norm_w,
                   lam_vecs, subln_w, w_out, lambda_init, cos, sin):
    b, s, _ = hn.shape
    proj = hn @ w_in
    z, xbc, dt, q, k, v, g = jnp.split(proj, EVEN_SPLITS, axis=-1)
    xbc = jax.nn.silu(causal_dwconv(xbc, conv_w, conv_b))
    xs, bm, cm = jnp.split(xbc, [SSD_WIDTH, SSD_WIDTH + SSD_GROUPS * SSD_STATE], axis=-1)
    xs = xs.reshape(b, s, SSD_HEADS, SSD_HEAD_DIM)
    bm = bm.reshape(b, s, SSD_GROUPS, SSD_STATE)
    cm = cm.reshape(b, s, SSD_GROUPS, SSD_STATE)
    dt = jax.nn.softplus(dt + dt_bias)
    a_neg = -jnp.exp(a_log)
    y = ssd_scan(xs, dt, a_neg, bm, cm) + xs * d_skip[:, None]
    y = group_rmsnorm(y.reshape(b, s, SSD_WIDTH) * jax.nn.silu(z), ssd_norm_w, SSD_GROUPS)
    q = apply_rope(q.reshape(b, s, DIFF_HEADS, 2, DIFF_HEAD_DIM), cos, sin)
    k = apply_rope(k.reshape(b, s, DIFF_HEADS, 2, DIFF_HEAD_DIM), cos, sin)
    v = v.reshape(b, s, DIFF_HEADS, DIFF_V_DIM)
    lv = lam_vecs.astype(jnp.float32)
    lam = jnp.exp(jnp.sum(lv[0] * lv[1])) - jnp.exp(jnp.sum(lv[2] * lv[3])) + lambda_init
    o = diff_attention(q, k, v, lam)
    o = rmsnorm(o, subln_w) * (1.0 - lambda_init)
    o = o.reshape(b, s, DIFF_WIDTH) * jax.nn.silu(g)
    return jnp.concatenate([y, o], axis=-1) @ w_out


def conformer_conv_mixer(hn, w_in, conv_w, conv_b, ln_w, ln_b, w_out):
    proj = hn @ w_in
    u, ug, g = jnp.split(proj, 3, axis=-1)
    u = u * jax.nn.sigmoid(ug)
    u = causal_dwconv(u, conv_w, conv_b)
    u = jax.nn.silu(layernorm(u, ln_w, ln_b))
    return (u * jax.nn.silu(g)) @ w_out


def setup_inputs(seed: int = 0) -> dict:
    key = jax.random.key(seed)
    ks = jax.random.split(key, 24)
    f32 = jnp.float32

    def nrm(k, shape, scale):
        return jax.random.normal(k, shape, f32) * scale

    dt0 = jnp.exp(jax.random.uniform(ks[8], (N_EVEN, SSD_HEADS), f32)
                  * (math.log(0.1) - math.log(0.001)) + math.log(0.001))
    return {
        "x": nrm(ks[0], (BATCH, SEQ, D_MODEL), 1.0),
        "p": nrm(ks[1], (DEPTH, BATCH, SEQ, D_PLE), 1.0),
        "norm_w": 1.0 + nrm(ks[2], (DEPTH, D_MODEL), 0.01),
        "ple_w": nrm(ks[3], (DEPTH, D_PLE, D_MODEL), D_PLE ** -0.5),
        "ple_gate_w": nrm(ks[4], (DEPTH, D_MODEL, D_MODEL), D_MODEL ** -0.5),
        "even_w_in": nrm(ks[5], (N_EVEN, D_MODEL, EVEN_IN), D_MODEL ** -0.5),
        "ssd_conv_w": nrm(ks[6], (N_EVEN, SSD_CONV, SSD_CONV_CH), SSD_CONV ** -0.5),
        "ssd_conv_b": nrm(ks[7], (N_EVEN, SSD_CONV_CH), 0.01),
        "ssd_dt_bias": dt0 + jnp.log(-jnp.expm1(-dt0)),
        "ssd_a_log": jnp.log(jax.random.uniform(ks[9], (N_EVEN, SSD_HEADS), f32, 1.0, 16.0)),
        "ssd_d": 1.0 + nrm(ks[10], (N_EVEN, SSD_HEADS), 0.1),
        "ssd_norm_w": 1.0 + nrm(ks[11], (N_EVEN, SSD_WIDTH), 0.01),
        "diff_lambda": nrm(ks[12], (N_EVEN, 4, DIFF_HEAD_DIM), 0.1),
        "diff_subln_w": 1.0 + nrm(ks[13], (N_EVEN, DIFF_V_DIM), 0.01),
        "even_w_out": nrm(ks[14], (N_EVEN, EVEN_MIX, D_MODEL), EVEN_MIX ** -0.5),
        "conf_w_in": nrm(ks[15], (N_ODD, D_MODEL, CONF_IN), D_MODEL ** -0.5),
        "conf_conv_w": nrm(ks[16], (N_ODD, CONF_KERNEL, CONF_WIDTH), CONF_KERNEL ** -0.5),
        "conf_conv_b": nrm(ks[17], (N_ODD, CONF_WIDTH), 0.01),
        "conf_ln_w": 1.0 + nrm(ks[18], (N_ODD, CONF_WIDTH), 0.01),
        "conf_ln_b": nrm(ks[19], (N_ODD, CONF_WIDTH), 0.01),
        "conf_w_out": nrm(ks[20], (N_ODD, CONF_WIDTH, D_MODEL), CONF_WIDTH ** -0.5),
        "final_norm_w": 1.0 + nrm(ks[21], (D_MODEL,), 0.01),
    }


def reference(x, p, norm_w, ple_w, ple_gate_w, even_w_in, ssd_conv_w, ssd_conv_b,
              ssd_dt_bias, ssd_a_log, ssd_d, ssd_norm_w, diff_lambda, diff_subln_w,
              even_w_out, conf_w_in, conf_conv_w, conf_conv_b, conf_ln_w, conf_ln_b,
              conf_w_out, final_norm_w):
    cos, sin = rope_tables(x.shape[1], DIFF_HEAD_DIM)
    h = x
    for i in range(DEPTH):
        hn = rmsnorm(h, norm_w[i])
        j = i // 2
        if i % 2 == 0:
            lambda_init = 0.8 - 0.6 * math.exp(-0.3 * i)
            h = h + ssd_diff_mixer(hn, even_w_in[j], ssd_conv_w[j], ssd_conv_b[j],
                                   ssd_dt_bias[j], ssd_a_log[j], ssd_d[j], ssd_norm_w[j],
                                   diff_lambda[j], diff_subln_w[j], even_w_out[j],
                                   lambda_init, cos, sin)
        else:
            h = h + conformer_conv_mixer(hn, conf_w_in[j], conf_conv_w[j], conf_conv_b[j],
                                         conf_ln_w[j], conf_ln_b[j], conf_w_out[j])
        e = p[i] @ ple_w[i]
        h = h + e * jax.nn.sigmoid(h @ ple_gate_w[i])
    return rmsnorm(h, final_norm_w)
```

```python
import functools
import math

import jax
import jax.numpy as jnp
from jax import lax
from jax.experimental import pallas as pl
from jax.experimental.pallas import tpu as pltpu

F32 = jnp.float32
BF16 = jnp.bfloat16

NORM_EPS = 1e-6
ROPE_THETA = 10000.0

SSD_HEADS = 16
SSD_HEAD_DIM = 64
SSD_WIDTH = SSD_HEADS * SSD_HEAD_DIM
SSD_GROUPS = 2
SSD_STATE = 128
SSD_CONV = 4
SSD_CHUNK = 128
SSD_BC = SSD_GROUPS * SSD_STATE
SSD_CONV_CH = SSD_WIDTH + 2 * SSD_BC
SSD_GROUP_WIDTH = SSD_WIDTH // SSD_GROUPS

DIFF_HEADS = 8
DIFF_HEAD_DIM = 64
DIFF_V_DIM = 2 * DIFF_HEAD_DIM
DIFF_WIDTH = DIFF_HEADS * DIFF_V_DIM

CONF_KERNEL = 31

LANES = 128
SUBLANES = 8
VMEM_LIMIT = 48 * 1024 * 1024
MASK_VALUE = -1e30

HIGHEST = lax.Precision.HIGHEST


def _silu(x):
    return x * jax.nn.sigmoid(x)


def _rms_rows(x, w):
    ms = jnp.mean(x * x, axis=-1, keepdims=True)
    return x * lax.rsqrt(ms + NORM_EPS) * w


def _const_spec(shape):
    return pl.BlockSpec(shape, lambda *_: (0,) * len(shape))


def _params(sem):
    return pltpu.CompilerParams(dimension_semantics=sem, vmem_limit_bytes=VMEM_LIMIT)


def _proj0_kernel(x_ref, nw_ref, cos_ref, sin_ref, wz, wxbc, wdt, wq, wk, wv, wg,
                  z_o, xbc_o, dt_o, q_o, k_o, v_o, g_o, *, q_scale):
    hn = _rms_rows(x_ref[...], nw_ref[...]).astype(BF16)

    def mm(w_ref):
        return jnp.dot(hn, w_ref[...], preferred_element_type=F32)

    z_o[...] = mm(wz).astype(BF16)
    xbc_o[...] = mm(wxbc).astype(BF16)
    dt_o[...] = mm(wdt)
    v_o[...] = mm(wv).astype(BF16)
    g_o[...] = mm(wg).astype(BF16)

    width = q_o.shape[-1]
    reps = width // LANES
    cos = jnp.tile(cos_ref[...], (1, reps))
    sin = jnp.tile(sin_ref[...], (1, reps))
    lane = lax.broadcasted_iota(jnp.int32, (1, width), 1)
    first_half = (lane % DIFF_HEAD_DIM) < (DIFF_HEAD_DIM // 2)

    def rope(t):
        partner = jnp.where(first_half,
                            pltpu.roll(t, width - DIFF_HEAD_DIM // 2, axis=1),
                            pltpu.roll(t, DIFF_HEAD_DIM // 2, axis=1))
        return t * cos + partner * sin

    q_o[...] = (rope(mm(wq)) * q_scale).astype(BF16)
    k_o[...] = rope(mm(wk)).astype(BF16)


def _proj0(x2, norm_w, cos_t, sin_t, ws, *, seq, tm):
    t, d = x2.shape
    wz, wxbc, wdt, wq, wk, wv, wg = ws
    nblk_seq = seq // tm
    row = lambda n: pl.BlockSpec((tm, n), lambda i: (i, 0))
    pos = pl.BlockSpec((tm, LANES), lambda i: (i % nblk_seq, 0))
    outs = [(SSD_WIDTH, BF16), (SSD_CONV_CH, BF16), (LANES, F32), (DIFF_WIDTH, BF16),
            (DIFF_WIDTH, BF16), (DIFF_WIDTH, BF16), (DIFF_WIDTH, BF16)]
    return pl.pallas_call(
        functools.partial(_proj0_kernel, q_scale=DIFF_HEAD_DIM ** -0.5),
        grid=(t // tm,),
        in_specs=[row(d), _const_spec((1, d)), pos, pos] + [_const_spec(w.shape) for w in ws],
        out_specs=[row(n) for n, _ in outs],
        out_shape=[jax.ShapeDtypeStruct((t, n), dt) for n, dt in outs],
        compiler_params=_params(("parallel",)),
        name="proj0",
    )(x2, norm_w, cos_t, sin_t, *ws)


def _ssd_kernel(xbc_ref, dt_ref, z_ref, cw_ref, cb_ref, dtb_ref, aneg_ref, anegx_ref,
                dskip_ref, nw_ref, expand_ref, y_ref, xpad_sc, state_sc):
    L = SSD_CHUNK
    halo = SUBLANES

    @pl.when(pl.program_id(1) == 0)
    def _():
        xpad_sc[0:halo, :] = jnp.zeros((halo, SSD_CONV_CH), F32)
        state_sc[...] = jnp.zeros_like(state_sc)

    xcur = xbc_ref[0].astype(F32)
    xpad_sc[halo:halo + L, :] = xcur
    acc = jnp.broadcast_to(cb_ref[...], (L, SSD_CONV_CH))
    for j in range(SSD_CONV):
        off = halo - (SSD_CONV - 1) + j
        acc = acc + cw_ref[j:j + 1, :] * xpad_sc[off:off + L, :]
    xpad_sc[0:halo, :] = xcur[L - halo:L, :]
    xbc = _silu(acc)
    xs = xbc[:, :SSD_WIDTH]
    bm = xbc[:, SSD_WIDTH:SSD_WIDTH + SSD_BC].astype(BF16)
    cm = xbc[:, SSD_WIDTH + SSD_BC:].astype(BF16)

    dt = jax.nn.softplus(dt_ref[0] + dtb_ref[...])
    a = dt * aneg_ref[...]
    r_i = lax.broadcasted_iota(jnp.int32, (L, L), 0)
    c_i = lax.broadcasted_iota(jnp.int32, (L, L), 1)
    causal = r_i >= c_i
    tril = causal.astype(F32)
    acs = jnp.dot(tril, a, precision=HIGHEST, preferred_element_type=F32)
    acs_t = acs.T
    dt_x = jnp.dot(dt, expand_ref[...], precision=HIGHEST, preferred_element_type=F32)
    acs_x = jnp.dot(tril, dt_x * anegx_ref[...], precision=HIGHEST, preferred_element_type=F32)
    acs_last = acs_x[L - 1:L, :]

    xd = xs * dt_x
    xd_bf = xd.astype(BF16)
    xdd_bf = (xd * jnp.exp(acs_last - acs_x)).astype(BF16)
    grow = jnp.exp(acs_x)
    chunk_decay = jnp.exp(acs_last)

    nt = (((1,), (1,)), ((), ()))
    tn = (((0,), (0,)), ((), ()))
    heads_per_group = SSD_HEADS // SSD_GROUPS
    quad_w = 2 * LANES
    heads_per_quad = quad_w // SSD_HEAD_DIM
    lane_q = lax.broadcasted_iota(jnp.int32, (1, quad_w), 1) // SSD_HEAD_DIM

    y_parts = []
    for g in range(SSD_GROUPS):
        bg = bm[:, g * SSD_STATE:(g + 1) * SSD_STATE]
        cg = cm[:, g * SSD_STATE:(g + 1) * SSD_STATE]
        gs = slice(g * SSD_GROUP_WIDTH, (g + 1) * SSD_GROUP_WIDTH)
        cb = lax.dot_general(cg, bg, nt, preferred_element_type=F32)
        prev = state_sc[g]
        y_g = jnp.dot(cg, prev.astype(BF16), preferred_element_type=F32) * grow[:, gs]
        new = lax.dot_general(bg, xdd_bf[:, gs], tn, preferred_element_type=F32)
        state_sc[g] = prev * chunk_decay[:, gs] + new
        quads = []
        for qd in range(SSD_GROUP_WIDTH // quad_w):
            lo = g * SSD_GROUP_WIDTH + qd * quad_w
            x_q = xd_bf[:, lo:lo + quad_w]
            y_q = jnp.zeros((L, quad_w), F32)
            for hq in range(heads_per_quad):
                h = g * heads_per_group + qd * heads_per_quad + hq
                diff = jnp.broadcast_to(acs[:, h:h + 1], (L, L)) - acs_t[h:h + 1, :]
                lmat = jnp.where(causal, jnp.exp(jnp.where(causal, diff, 0.0)), 0.0)
                m = (cb * lmat).astype(BF16)
                r = jnp.dot(m, x_q, preferred_element_type=F32)
                y_q = jnp.where(lane_q == hq, r, y_q)
            quads.append(y_q)
        y_parts.append(y_g + jnp.concatenate(quads, axis=1))

    z = z_ref[0].astype(F32)
    outs = []
    for g in range(SSD_GROUPS):
        gs = slice(g * SSD_GROUP_WIDTH, (g + 1) * SSD_GROUP_WIDTH)
        yz = (y_parts[g] + xs[:, gs] * dskip_ref[:, gs]) * _silu(z[:, gs])
        outs.append(_rms_rows(yz, nw_ref[:, gs]))
    y_ref[0] = jnp.concatenate(outs, axis=1).astype(BF16)


def _ssd(xbc, dt, z, conv_w, conv_b, dt_bias, a_neg, a_neg_x, d_skip_x, norm_w, expand):
    b, s, _ = xbc.shape
    L = SSD_CHUNK
    blk = lambda n: pl.BlockSpec((1, L, n), lambda bi, ci: (bi, ci, 0))
    consts = [conv_w, conv_b, dt_bias, a_neg, a_neg_x, d_skip_x, norm_w, expand]
    return pl.pallas_call(
        _ssd_kernel,
        grid=(b, s // L),
        in_specs=[blk(SSD_CONV_CH), blk(LANES), blk(SSD_WIDTH)] + [_const_spec(c.shape) for c in consts],
        out_specs=blk(SSD_WIDTH),
        out_shape=jax.ShapeDtypeStruct((b, s, SSD_WIDTH), BF16),
        scratch_shapes=[pltpu.VMEM((SUBLANES + L, SSD_CONV_CH), F32),
                        pltpu.VMEM((SSD_GROUPS, SSD_STATE, SSD_GROUP_WIDTH), F32)],
        compiler_params=_params(("parallel", "arbitrary")),
        name="ssd",
    )(xbc, dt, z, *consts)


def _attn_kernel(lam_ref, q_ref, k_ref, v_ref, g_ref, sw_ref, o_ref,
                 m_sc, l_sc, acc_sc, *, tq, lambda_init):
    qi = pl.program_id(2)
    nt = (((1,), (1,)), ((), ()))
    tn = (((0,), (0,)), ((), ()))

    q = q_ref[0]
    lane = lax.broadcasted_iota(jnp.int32, q.shape, 1)
    zero = jnp.zeros_like(q)
    q_maps = (jnp.where(lane < DIFF_HEAD_DIM, q, zero), jnp.where(lane >= DIFF_HEAD_DIM, q, zero))

    m_sc[...] = jnp.full(m_sc.shape, MASK_VALUE, F32)
    l_sc[...] = jnp.zeros(l_sc.shape, F32)
    acc_sc[...] = jnp.zeros(acc_sc.shape, F32)

    def tile(j, masked):
        start = pl.multiple_of(j * tq, tq)
        k = k_ref[0, pl.ds(start, tq), :]
        v = v_ref[0, pl.ds(start, tq), :]
        for c in range(2):
            s = lax.dot_general(k, q_maps[c], nt, preferred_element_type=F32)
            if masked:
                kpos = lax.broadcasted_iota(jnp.int32, s.shape, 0)
                qpos = lax.broadcasted_iota(jnp.int32, s.shape, 1)
                s = jnp.where(kpos <= qpos, s, MASK_VALUE)
            m_old = m_sc[c]
            m_new = jnp.maximum(m_old, jnp.max(s, axis=0, keepdims=True))
            p = jnp.exp(s - m_new)
            alpha = jnp.exp(m_old - m_new)
            l_sc[c] = alpha * l_sc[c] + jnp.sum(p, axis=0, keepdims=True)
            pv = lax.dot_general(v, p.astype(BF16), tn, preferred_element_type=F32)
            acc_sc[c] = alpha * acc_sc[c] + pv
            m_sc[c] = m_new

    def body(j, carry):
        tile(j, False)
        return carry

    lax.fori_loop(0, qi, body, 0)
    tile(qi, True)

    lv = lam_ref[...]
    lam = (jnp.exp(jnp.sum(lv[0:1] * lv[1:2], axis=1, keepdims=True))
           - jnp.exp(jnp.sum(lv[2:3] * lv[3:4], axis=1, keepdims=True)) + lambda_init)
    o_t = acc_sc[0] / l_sc[0] - lam * (acc_sc[1] / l_sc[1])
    o = _rms_rows(o_t.T, sw_ref[...]) * (1.0 - lambda_init)
    o_ref[0] = (o * _silu(g_ref[0].astype(F32))).astype(BF16)


def _attn(q, k, v, g, lam_vecs, subln_w, *, lambda_init, tq):
    b, s, _ = q.shape
    qblk = pl.BlockSpec((1, tq, DIFF_V_DIM), lambda bi, hi, qi: (bi, qi, hi))
    kvblk = pl.BlockSpec((1, s, DIFF_V_DIM), lambda bi, hi, qi: (bi, 0, hi))
    return pl.pallas_call(
        functools.partial(_attn_kernel, tq=tq, lambda_init=lambda_init),
        grid=(b, DIFF_HEADS, s // tq),
        in_specs=[_const_spec(lam_vecs.shape), qblk, kvblk, kvblk, qblk, _const_spec(subln_w.shape)],
        out_specs=qblk,
        out_shape=jax.ShapeDtypeStruct((b, s, DIFF_WIDTH), BF16),
        scratch_shapes=[pltpu.VMEM((2, 1, tq), F32), pltpu.VMEM((2, 1, tq), F32),
                        pltpu.VMEM((2, DIFF_V_DIM, tq), F32)],
        compiler_params=_params(("parallel", "parallel", "arbitrary")),
        name="diff_attn",
    )(lam_vecs, q, k, v, g, subln_w)


def _ple(h, p_ref, pw_ref, gw_ref):
    e = jnp.dot(p_ref[...].astype(BF16), pw_ref[...], preferred_element_type=F32)
    gate = jax.nn.sigmoid(jnp.dot(h.astype(BF16), gw_ref[...], preferred_element_type=F32))
    return h + e * gate


def _out0_kernel(y_ref, o_ref, x_ref, p_ref, wy_ref, wo_ref, pw_ref, gw_ref, h_ref):
    mix = (jnp.dot(y_ref[...], wy_ref[...], preferred_element_type=F32)
           + jnp.dot(o_ref[...], wo_ref[...], preferred_element_type=F32))
    h_ref[...] = _ple(x_ref[...] + mix, p_ref, pw_ref, gw_ref)


def _out0(y2, o2, x2, p2, wy, wo, pw, gw, *, tm):
    t, d = x2.shape
    row = lambda n: pl.BlockSpec((tm, n), lambda i: (i, 0))
    ws = [wy, wo, pw, gw]
    return pl.pallas_call(
        _out0_kernel,
        grid=(t // tm,),
        in_specs=[row(y2.shape[1]), row(o2.shape[1]), row(d), row(p2.shape[1])]
                 + [_const_spec(w.shape) for w in ws],
        out_specs=row(d),
        out_shape=jax.ShapeDtypeStruct((t, d), F32),
        compiler_params=_params(("parallel",)),
        name="out0",
    )(y2, o2, x2, p2, *ws)


def _proj1_kernel(h_ref, nw_ref, wu, wug, wg, u_o, g_o):
    hn = _rms_rows(h_ref[...], nw_ref[...]).astype(BF16)
    u = jnp.dot(hn, wu[...], preferred_element_type=F32)
    ug = jnp.dot(hn, wug[...], preferred_element_type=F32)
    u_o[...] = (u * jax.nn.sigmoid(ug)).astype(BF16)
    g_o[...] = jnp.dot(hn, wg[...], preferred_element_type=F32).astype(BF16)


def _proj1(h2, norm_w, wu, wug, wg, *, tm):
    t, d = h2.shape
    cw = wu.shape[1]
    row = lambda n: pl.BlockSpec((tm, n), lambda i: (i, 0))
    ws = [wu, wug, wg]
    return pl.pallas_call(
        _proj1_kernel,
        grid=(t // tm,),
        in_specs=[row(d), _const_spec((1, d))] + [_const_spec(w.shape) for w in ws],
        out_specs=[row(cw), row(cw)],
        out_shape=[jax.ShapeDtypeStruct((t, cw), BF16)] * 2,
        compiler_params=_params(("parallel",)),
        name="proj1",
    )(h2, norm_w, *ws)


def _conv_kernel(u_ref, halo_ref, g_ref, h_ref, p_ref, cw_ref, cb_ref, lnw_ref, lnb_ref,
                 wo_ref, pw_ref, gw_ref, fw_ref, out_ref, ubuf_sc, conv_sc, *, tm, halo, cblk):
    width = u_ref.shape[-1]
    first = pl.program_id(1) == 0
    halo_rows = halo_ref[0].astype(F32)
    ubuf_sc[0:halo, :] = jnp.where(first, jnp.zeros_like(halo_rows), halo_rows)
    ubuf_sc[halo:halo + tm, :] = u_ref[0].astype(F32)

    base = halo - (CONF_KERNEL - 1)
    for cb in range(width // cblk):
        cs = slice(cb * cblk, (cb + 1) * cblk)
        acc = jnp.broadcast_to(cb_ref[:, cs], (tm, cblk))
        for j in range(CONF_KERNEL):
            acc = acc + cw_ref[j:j + 1, cs] * ubuf_sc[base + j:base + j + tm, cs]
        conv_sc[:, cs] = acc

    c = conv_sc[...]
    mu = jnp.mean(c, axis=-1, keepdims=True)
    cc = c - mu
    var = jnp.mean(cc * cc, axis=-1, keepdims=True)
    ln = cc * lax.rsqrt(var + NORM_EPS) * lnw_ref[...] + lnb_ref[...]
    act = (_silu(ln) * _silu(g_ref[0].astype(F32))).astype(BF16)
    h = h_ref[0] + jnp.dot(act, wo_ref[...], preferred_element_type=F32)
    h = _ple(h, p_ref.at[0], pw_ref, gw_ref)
    out_ref[0] = _rms_rows(h, fw_ref[...])


def _conv(u, g, h, p, conv_w, conv_b, ln_w, ln_b, wo, pw, gw, fw, *, tm, halo, cblk):
    b, s, cw = u.shape
    d = h.shape[-1]
    per = tm // halo
    blk = lambda n: pl.BlockSpec((1, tm, n), lambda bi, i: (bi, i, 0))
    halo_spec = pl.BlockSpec((1, halo, cw), lambda bi, i: (bi, jnp.maximum(i * per - 1, 0), 0))
    consts = [conv_w, conv_b, ln_w, ln_b, wo, pw, gw, fw]
    return pl.pallas_call(
        functools.partial(_conv_kernel, tm=tm, halo=halo, cblk=cblk),
        grid=(b, s // tm),
        in_specs=[blk(cw), halo_spec, blk(cw), blk(d), blk(p.shape[-1])]
                 + [_const_spec(c.shape) for c in consts],
        out_specs=blk(d),
        out_shape=jax.ShapeDtypeStruct((b, s, d), F32),
        scratch_shapes=[pltpu.VMEM((halo + tm, cw), F32), pltpu.VMEM((tm, cw), F32)],
        compiler_params=_params(("parallel", "parallel")),
        name="conv_tail",
    )(u, u, g, h, p, *consts)


def _rope_tables(seq):
    half = DIFF_HEAD_DIM // 2
    pos = jnp.arange(seq, dtype=F32)
    inv = ROPE_THETA ** (-jnp.arange(0, DIFF_HEAD_DIM, 2, dtype=F32) / DIFF_HEAD_DIM)
    ang = pos[:, None] * inv[None, :]
    cos, sin = jnp.cos(ang), jnp.sin(ang)
    reps = LANES // DIFF_HEAD_DIM
    cos_t = jnp.tile(jnp.concatenate([cos, cos], axis=1), (1, reps))
    sin_t = jnp.tile(jnp.concatenate([-sin, sin], axis=1), (1, reps))
    return cos_t, sin_t


def _pick(n, pref):
    while n % pref:
        pref //= 2
    return pref


def kernel(x, p, norm_w, ple_w, ple_gate_w, even_w_in, ssd_conv_w, ssd_conv_b, ssd_dt_bias, ssd_a_log, ssd_d, ssd_norm_w, diff_lambda, diff_subln_w, even_w_out, conf_w_in, conf_conv_w, conf_conv_b, conf_ln_w, conf_ln_b, conf_w_out, final_norm_w):
    b, s, d = x.shape
    t = b * s
    x2 = x.reshape(t, d)
    row = lambda v: v.reshape(1, -1).astype(F32)

    w_in = even_w_in[0].astype(BF16)
    o_z, o_xbc = SSD_WIDTH, SSD_WIDTH + SSD_CONV_CH
    o_dt = o_xbc + SSD_HEADS
    o_q, o_k, o_v = o_dt + DIFF_WIDTH, o_dt + 2 * DIFF_WIDTH, o_dt + 3 * DIFF_WIDTH
    w_dt = jnp.pad(w_in[:, o_xbc:o_dt], ((0, 0), (0, LANES - SSD_HEADS)))
    ws0 = (w_in[:, :o_z], w_in[:, o_z:o_xbc], w_dt, w_in[:, o_dt:o_q], w_in[:, o_q:o_k],
           w_in[:, o_k:o_v], w_in[:, o_v:])
    cos_t, sin_t = _rope_tables(s)
    tm0 = _pick(s, 256)
    z, xbc, dt, q, k, v, g = _proj0(x2, row(norm_w[0]), cos_t, sin_t, ws0, seq=s, tm=tm0)

    head_of_channel = jnp.arange(SSD_WIDTH) // SSD_HEAD_DIM
    expand = (jnp.arange(LANES)[:, None] == head_of_channel[None, :]).astype(F32)
    a_neg = -jnp.exp(ssd_a_log[0].astype(F32))
    pad_h = lambda vec: jnp.pad(vec.astype(F32), (0, LANES - SSD_HEADS)).reshape(1, LANES)
    y = _ssd(xbc.reshape(b, s, -1), dt.reshape(b, s, -1), z.reshape(b, s, -1),
             ssd_conv_w[0].astype(F32), row(ssd_conv_b[0]), pad_h(ssd_dt_bias[0]), pad_h(a_neg),
             row(a_neg[head_of_channel]), row(ssd_d[0][head_of_channel]), row(ssd_norm_w[0]), expand)

    lambda_init = 0.8 - 0.6 * math.exp(-0.3 * 0)
    o = _attn(q.reshape(b, s, -1), k.reshape(b, s, -1), v.reshape(b, s, -1), g.reshape(b, s, -1),
              diff_lambda[0].astype(F32), row(diff_subln_w[0]),
              lambda_init=lambda_init, tq=_pick(s, 256))

    w_out = even_w_out[0].astype(BF16)
    h1 = _out0(y.reshape(t, -1), o.reshape(t, -1), x2, p[0].reshape(t, -1),
               w_out[:SSD_WIDTH], w_out[SSD_WIDTH:], ple_w[0].astype(BF16),
               ple_gate_w[0].astype(BF16), tm=_pick(t, 512))

    w1 = conf_w_in[0].astype(BF16)
    cw = w1.shape[1] // 3
    u, g1 = _proj1(h1, row(norm_w[1]), w1[:, :cw], w1[:, cw:2 * cw], w1[:, 2 * cw:], tm=_pick(t, 256))
    halo = 32
    conv_w = jnp.pad(conf_conv_w[0].astype(F32), ((0, halo - CONF_KERNEL), (0, 0)))
    out = _conv(u.reshape(b, s, cw), g1.reshape(b, s, cw), h1.reshape(b, s, d), p[1],
                conv_w, row(conf_conv_b[0]), row(conf_ln_w[0]), row(conf_ln_b[0]),
                conf_w_out[0].astype(BF16), ple_w[1].astype(BF16), ple_gate_w[1].astype(BF16),
                row(final_norm_w), tm=_pick(s, 256), halo=halo, cblk=256)
    return out
```

```python
import functools
import math

import jax
import jax.numpy as jnp
from jax import lax
from jax.experimental import pallas as pl
from jax.experimental.pallas import tpu as pltpu

F32 = jnp.float32
BF16 = jnp.bfloat16

NORM_EPS = 1e-6
ROPE_THETA = 10000.0

SSD_HEADS = 16
SSD_HEAD_DIM = 64
SSD_WIDTH = SSD_HEADS * SSD_HEAD_DIM
SSD_GROUPS = 2
SSD_STATE = 128
SSD_CONV = 4
SSD_CHUNK = 128
SSD_BC = SSD_GROUPS * SSD_STATE
SSD_CONV_CH = SSD_WIDTH + 2 * SSD_BC
SSD_GROUP_WIDTH = SSD_WIDTH // SSD_GROUPS

DIFF_HEADS = 8
DIFF_HEAD_DIM = 64
DIFF_V_DIM = 2 * DIFF_HEAD_DIM
DIFF_WIDTH = DIFF_HEADS * DIFF_V_DIM

CONF_KERNEL = 31

LANES = 128
SUBLANES = 8
VMEM_LIMIT = 48 * 1024 * 1024
MASK_VALUE = -1e30
ATTN_TILE = 512
CONV_TILE = 256
CONV_HALO = 32
CONV_ROW_BLOCK = 128

HIGHEST = lax.Precision.HIGHEST


def _silu(x):
    return x * jax.nn.sigmoid(x)


def _rms_rows(x, w):
    ms = jnp.mean(x * x, axis=-1, keepdims=True)
    return x * lax.rsqrt(ms + NORM_EPS) * w


def _const_spec(shape):
    return pl.BlockSpec(shape, lambda *_: (0,) * len(shape))


def _params(sem):
    return pltpu.CompilerParams(dimension_semantics=sem, vmem_limit_bytes=VMEM_LIMIT)


def _proj0_kernel(x_ref, nw_ref, cos_ref, sin_ref, wz, wxbc, wdt, wq, wk, wv, wg,
                  z_o, xbc_o, dt_o, q_o, k_o, v_o, g_o, *, q_scale):
    hn = _rms_rows(x_ref[...], nw_ref[...]).astype(BF16)

    def mm(w_ref):
        return jnp.dot(hn, w_ref[...], preferred_element_type=F32)

    z_o[...] = mm(wz).astype(BF16)
    xbc_o[...] = mm(wxbc).astype(BF16)
    dt_o[...] = mm(wdt)
    v_o[...] = mm(wv).astype(BF16)
    g_o[...] = mm(wg).astype(BF16)

    width = q_o.shape[-1]
    reps = width // LANES
    cos = jnp.tile(cos_ref[...], (1, reps))
    sin = jnp.tile(sin_ref[...], (1, reps))
    lane = lax.broadcasted_iota(jnp.int32, (1, width), 1)
    first_half = (lane % DIFF_HEAD_DIM) < (DIFF_HEAD_DIM // 2)

    def rope(t):
        partner = jnp.where(first_half,
                            pltpu.roll(t, width - DIFF_HEAD_DIM // 2, axis=1),
                            pltpu.roll(t, DIFF_HEAD_DIM // 2, axis=1))
        return t * cos + partner * sin

    q_o[...] = (rope(mm(wq)) * q_scale).astype(BF16)
    k_o[...] = rope(mm(wk)).astype(BF16)


def _proj0(x2, norm_w, cos_t, sin_t, ws, *, seq, tm):
    t, d = x2.shape
    wz, wxbc, wdt, wq, wk, wv, wg = ws
    nblk_seq = seq // tm
    row = lambda n: pl.BlockSpec((tm, n), lambda i: (i, 0))
    pos = pl.BlockSpec((tm, LANES), lambda i: (i % nblk_seq, 0))
    outs = [(SSD_WIDTH, BF16), (SSD_CONV_CH, BF16), (LANES, F32), (DIFF_WIDTH, BF16),
            (DIFF_WIDTH, BF16), (DIFF_WIDTH, BF16), (DIFF_WIDTH, BF16)]
    return pl.pallas_call(
        functools.partial(_proj0_kernel, q_scale=DIFF_HEAD_DIM ** -0.5 * math.log2(math.e)),
        grid=(t // tm,),
        in_specs=[row(d), _const_spec((1, d)), pos, pos] + [_const_spec(w.shape) for w in ws],
        out_specs=[row(n) for n, _ in outs],
        out_shape=[jax.ShapeDtypeStruct((t, n), dt) for n, dt in outs],
        compiler_params=_params(("parallel",)),
        name="proj0",
    )(x2, norm_w, cos_t, sin_t, *ws)


def _ssd_kernel(xbc_ref, dt_ref, z_ref, cw_ref, cb_ref, dtb_ref, aneg_ref, anegx_ref,
                dskip_ref, nw_ref, expand_ref, y_ref, xpad_sc, state_sc):
    L = SSD_CHUNK
    halo = SUBLANES

    @pl.when(pl.program_id(1) == 0)
    def _():
        xpad_sc[0:halo, :] = jnp.zeros((halo, SSD_CONV_CH), F32)
        state_sc[...] = jnp.zeros_like(state_sc)

    xcur = xbc_ref[0].astype(F32)
    xpad_sc[halo:halo + L, :] = xcur
    acc = jnp.broadcast_to(cb_ref[...], (L, SSD_CONV_CH))
    for j in range(SSD_CONV):
        off = halo - (SSD_CONV - 1) + j
        acc = acc + cw_ref[j:j + 1, :] * xpad_sc[off:off + L, :]
    xpad_sc[0:halo, :] = xcur[L - halo:L, :]
    xbc = _silu(acc)
    xs = xbc[:, :SSD_WIDTH]
    bm = xbc[:, SSD_WIDTH:SSD_WIDTH + SSD_BC].astype(BF16)
    cm = xbc[:, SSD_WIDTH + SSD_BC:].astype(BF16)

    dt = jax.nn.softplus(dt_ref[0] + dtb_ref[...])
    a = dt * aneg_ref[...]
    r_i = lax.broadcasted_iota(jnp.int32, (L, L), 0)
    c_i = lax.broadcasted_iota(jnp.int32, (L, L), 1)
    causal = r_i >= c_i
    tril = causal.astype(F32)
    acs = jnp.dot(tril, a, precision=HIGHEST, preferred_element_type=F32)
    acs_t = acs.T
    dt_x = jnp.dot(dt, expand_ref[...], precision=HIGHEST, preferred_element_type=F32)
    acs_x = jnp.dot(tril, dt_x * anegx_ref[...], precision=HIGHEST, preferred_element_type=F32)
    acs_last = acs_x[L - 1:L, :]

    xd = xs * dt_x
    xd_bf = xd.astype(BF16)
    xdd_bf = (xd * jnp.exp(acs_last - acs_x)).astype(BF16)
    grow = jnp.exp(acs_x)
    chunk_decay = jnp.exp(acs_last)

    nt = (((1,), (1,)), ((), ()))
    tn = (((0,), (0,)), ((), ()))
    heads_per_group = SSD_HEADS // SSD_GROUPS
    quad_w = 2 * LANES
    heads_per_quad = quad_w // SSD_HEAD_DIM
    lane_q = lax.broadcasted_iota(jnp.int32, (1, quad_w), 1) // SSD_HEAD_DIM

    y_parts = []
    for g in range(SSD_GROUPS):
        bg = bm[:, g * SSD_STATE:(g + 1) * SSD_STATE]
        cg = cm[:, g * SSD_STATE:(g + 1) * SSD_STATE]
        gs = slice(g * SSD_GROUP_WIDTH, (g + 1) * SSD_GROUP_WIDTH)
        cb = lax.dot_general(cg, bg, nt, preferred_element_type=F32)
        prev = state_sc[g]
        y_g = jnp.dot(cg, prev.astype(BF16), preferred_element_type=F32) * grow[:, gs]
        new = lax.dot_general(bg, xdd_bf[:, gs], tn, preferred_element_type=F32)
        state_sc[g] = prev * chunk_decay[:, gs] + new
        quads = []
        for qd in range(SSD_GROUP_WIDTH // quad_w):
            lo = g * SSD_GROUP_WIDTH + qd * quad_w
            x_q = xd_bf[:, lo:lo + quad_w]
            y_q = jnp.zeros((L, quad_w), F32)
            for hq in range(heads_per_quad):
                h = g * heads_per_group + qd * heads_per_quad + hq
                diff = jnp.broadcast_to(acs[:, h:h + 1], (L, L)) - acs_t[h:h + 1, :]
                lmat = jnp.where(causal, jnp.exp(jnp.where(causal, diff, 0.0)), 0.0)
                m = (cb * lmat).astype(BF16)
                r = jnp.dot(m, x_q, preferred_element_type=F32)
                y_q = jnp.where(lane_q == hq, r, y_q)
            quads.append(y_q)
        y_parts.append(y_g + jnp.concatenate(quads, axis=1))

    z = z_ref[0].astype(F32)
    outs = []
    for g in range(SSD_GROUPS):
        gs = slice(g * SSD_GROUP_WIDTH, (g + 1) * SSD_GROUP_WIDTH)
        yz = (y_parts[g] + xs[:, gs] * dskip_ref[:, gs]) * _silu(z[:, gs])
        outs.append(_rms_rows(yz, nw_ref[:, gs]))
    y_ref[0] = jnp.concatenate(outs, axis=1).astype(BF16)


def _ssd(xbc, dt, z, conv_w, conv_b, dt_bias, a_neg, a_neg_x, d_skip_x, norm_w, expand):
    b, s, _ = xbc.shape
    L = SSD_CHUNK
    blk = lambda n: pl.BlockSpec((1, L, n), lambda bi, ci: (bi, ci, 0))
    consts = [conv_w, conv_b, dt_bias, a_neg, a_neg_x, d_skip_x, norm_w, expand]
    return pl.pallas_call(
        _ssd_kernel,
        grid=(b, s // L),
        in_specs=[blk(SSD_CONV_CH), blk(LANES), blk(SSD_WIDTH)] + [_const_spec(c.shape) for c in consts],
        out_specs=blk(SSD_WIDTH),
        out_shape=jax.ShapeDtypeStruct((b, s, SSD_WIDTH), BF16),
        scratch_shapes=[pltpu.VMEM((SUBLANES + L, SSD_CONV_CH), F32),
                        pltpu.VMEM((SSD_GROUPS, SSD_STATE, SSD_GROUP_WIDTH), F32)],
        compiler_params=_params(("parallel", "arbitrary")),
        name="ssd",
    )(xbc, dt, z, *consts)


def _attn_kernel(lam_ref, q_ref, k_ref, v_ref, g_ref, sw_ref, o_ref,
                 qm_sc, sa_sc, sb_sc, mxa_sc, mxb_sc, m_sc, l_sc, acc_sc, *, tq, lambda_init):
    qi = pl.program_id(2)
    nt = (((1,), (1,)), ((), ()))
    tn = (((0,), (0,)), ((), ()))

    q = q_ref[0]
    lane = lax.broadcasted_iota(jnp.int32, q.shape, 1)
    zero = jnp.zeros_like(q)
    qm_sc[0] = jnp.where(lane < DIFF_HEAD_DIM, q, zero)
    qm_sc[1] = jnp.where(lane >= DIFF_HEAD_DIM, q, zero)

    m_sc[...] = jnp.full(m_sc.shape, MASK_VALUE, F32)
    l_sc[...] = jnp.zeros(l_sc.shape, F32)
    acc_sc[...] = jnp.zeros(acc_sc.shape, F32)

    def scores(j, masked, s_ref, mx_ref):
        start = pl.multiple_of(j * tq, tq)
        k = k_ref[0, pl.ds(start, tq), :]
        for c in range(2):
            s = lax.dot_general(k, qm_sc[c], nt, preferred_element_type=F32)
            if masked:
                kpos = lax.broadcasted_iota(jnp.int32, s.shape, 0)
                qpos = lax.broadcasted_iota(jnp.int32, s.shape, 1)
                s = jnp.where(kpos <= qpos, s, MASK_VALUE)
            s_ref[c] = s
            mx_ref[c] = jnp.max(s, axis=0, keepdims=True)

    def consume(j, s_ref, mx_ref):
        start = pl.multiple_of(j * tq, tq)
        v = v_ref[0, pl.ds(start, tq), :]
        for c in range(2):
            m_old = m_sc[c]
            m_new = jnp.maximum(m_old, mx_ref[c])
            p = jnp.exp2(s_ref[c] - m_new)
            alpha = jnp.exp2(m_old - m_new)
            l_sc[c] = alpha * l_sc[c] + jnp.sum(p, axis=0, keepdims=True)
            pv = lax.dot_general(v, p.astype(BF16), tn, preferred_element_type=F32)
            acc_sc[c] = alpha * acc_sc[c] + pv
            m_sc[c] = m_new

    @pl.when(qi > 0)
    def _():
        scores(0, False, sa_sc, mxa_sc)

    def pair(i, carry):
        j = 2 * i
        scores(j + 1, False, sb_sc, mxb_sc)
        consume(j, sa_sc, mxa_sc)
        scores(j + 2, False, sa_sc, mxa_sc)
        consume(j + 1, sb_sc, mxb_sc)
        return carry

    npairs = jnp.maximum(qi - 1, 0) // 2
    lax.fori_loop(0, npairs, pair, 0)
    done = 2 * npairs
    rem = qi - done

    @pl.when(rem == 0)
    def _():
        scores(qi, True, sa_sc, mxa_sc)
        consume(qi, sa_sc, mxa_sc)

    @pl.when(rem == 1)
    def _():
        scores(qi, True, sb_sc, mxb_sc)
        consume(done, sa_sc, mxa_sc)
        consume(qi, sb_sc, mxb_sc)

    @pl.when(rem == 2)
    def _():
        scores(done + 1, False, sb_sc, mxb_sc)
        consume(done, sa_sc, mxa_sc)
        scores(qi, True, sa_sc, mxa_sc)
        consume(done + 1, sb_sc, mxb_sc)
        consume(qi, sa_sc, mxa_sc)

    lv = lam_ref[...]
    lam = (jnp.exp(jnp.sum(lv[0:1] * lv[1:2], axis=1, keepdims=True))
           - jnp.exp(jnp.sum(lv[2:3] * lv[3:4], axis=1, keepdims=True)) + lambda_init)
    o_t = acc_sc[0] / l_sc[0] - lam * (acc_sc[1] / l_sc[1])
    o = _rms_rows(o_t.T, sw_ref[...]) * (1.0 - lambda_init)
    o_ref[0] = (o * _silu(g_ref[0].astype(F32))).astype(BF16)


def _attn_entry(q, k, v, g, lam_vecs, subln_w):
    lambda_init = 0.8 - 0.6 * math.exp(-0.3 * 0)
    return _attn(q, k, v, g, lam_vecs.astype(F32), subln_w.reshape(1, -1).astype(F32),
                 lambda_init=lambda_init, tq=_pick(q.shape[1], ATTN_TILE))


def _attn(q, k, v, g, lam_vecs, subln_w, *, lambda_init, tq):
    b, s, _ = q.shape
    qblk = pl.BlockSpec((1, tq, DIFF_V_DIM), lambda bi, hi, qi: (bi, qi, hi))
    kvblk = pl.BlockSpec((1, s, DIFF_V_DIM), lambda bi, hi, qi: (bi, 0, hi))
    stat = pltpu.VMEM((2, 1, tq), F32)
    sbuf = pltpu.VMEM((2, tq, tq), F32)
    return pl.pallas_call(
        functools.partial(_attn_kernel, tq=tq, lambda_init=lambda_init),
        grid=(b, DIFF_HEADS, s // tq),
        in_specs=[_const_spec(lam_vecs.shape), qblk, kvblk, kvblk, qblk, _const_spec(subln_w.shape)],
        out_specs=qblk,
        out_shape=jax.ShapeDtypeStruct((b, s, DIFF_WIDTH), BF16),
        scratch_shapes=[pltpu.VMEM((2, tq, DIFF_V_DIM), BF16), sbuf, sbuf, stat, stat, stat, stat,
                        pltpu.VMEM((2, DIFF_V_DIM, tq), F32)],
        compiler_params=_params(("parallel", "parallel", "arbitrary")),
        name="diff_attn",
    )(lam_vecs, q, k, v, g, subln_w)


def _ple(h, p_ref, pw_ref, gw_ref):
    e = jnp.dot(p_ref[...].astype(BF16), pw_ref[...], preferred_element_type=F32)
    gate = jax.nn.sigmoid(jnp.dot(h.astype(BF16), gw_ref[...], preferred_element_type=F32))
    return h + e * gate


def _out0_kernel(y_ref, o_ref, x_ref, p_ref, wy_ref, wo_ref, pw_ref, gw_ref, h_ref):
    mix = (jnp.dot(y_ref[...], wy_ref[...], preferred_element_type=F32)
           + jnp.dot(o_ref[...], wo_ref[...], preferred_element_type=F32))
    h_ref[...] = _ple(x_ref[...] + mix, p_ref, pw_ref, gw_ref)


def _out0(y2, o2, x2, p2, wy, wo, pw, gw, *, tm):
    t, d = x2.shape
    row = lambda n: pl.BlockSpec((tm, n), lambda i: (i, 0))
    ws = [wy, wo, pw, gw]
    return pl.pallas_call(
        _out0_kernel,
        grid=(t // tm,),
        in_specs=[row(y2.shape[1]), row(o2.shape[1]), row(d), row(p2.shape[1])]
                 + [_const_spec(w.shape) for w in ws],
        out_specs=row(d),
        out_shape=jax.ShapeDtypeStruct((t, d), F32),
        compiler_params=_params(("parallel",)),
        name="out0",
    )(y2, o2, x2, p2, *ws)


def _proj1_kernel(h_ref, nw_ref, wu, wug, wg, u_o, g_o):
    hn = _rms_rows(h_ref[...], nw_ref[...]).astype(BF16)
    u = jnp.dot(hn, wu[...], preferred_element_type=F32)
    ug = jnp.dot(hn, wug[...], preferred_element_type=F32)
    u_o[...] = (u * jax.nn.sigmoid(ug)).astype(BF16)
    g_o[...] = jnp.dot(hn, wg[...], preferred_element_type=F32).astype(BF16)


def _proj1(h2, norm_w, wu, wug, wg, *, tm):
    t, d = h2.shape
    cw = wu.shape[1]
    row = lambda n: pl.BlockSpec((tm, n), lambda i: (i, 0))
    ws = [wu, wug, wg]
    return pl.pallas_call(
        _proj1_kernel,
        grid=(t // tm,),
        in_specs=[row(d), _const_spec((1, d))] + [_const_spec(w.shape) for w in ws],
        out_specs=[row(cw), row(cw)],
        out_shape=[jax.ShapeDtypeStruct((t, cw), BF16)] * 2,
        compiler_params=_params(("parallel",)),
        name="proj1",
    )(h2, norm_w, *ws)


def _conv_kernel(u_ref, halo_ref, g_ref, h_ref, p_ref, cw_ref, cb_ref, lnw_ref, lnb_ref,
                 wo_ref, pw_ref, gw_ref, fw_ref, out_ref, ubuf_sc, conv_sc, *, tm, halo, cblk):
    width = u_ref.shape[-1]
    first = pl.program_id(1) == 0
    halo_rows = halo_ref[0].astype(F32)
    ubuf_sc[0:halo, :] = jnp.where(first, jnp.zeros_like(halo_rows), halo_rows)
    ubuf_sc[halo:halo + tm, :] = u_ref[0].astype(F32)

    base = halo - (CONF_KERNEL - 1)
    rblk = min(tm, CONV_ROW_BLOCK)
    win_rows = rblk + halo

    def col_block(cb, carry):
        cs = pl.ds(pl.multiple_of(cb * cblk, cblk), cblk)
        for rb in range(tm // rblk):
            win = ubuf_sc[pl.ds(rb * rblk, win_rows), cs]
            acc = jnp.broadcast_to(cb_ref[:, cs], (rblk, cblk))
            for sh in range(SUBLANES):
                shifted = win if sh == 0 else pltpu.roll(win, win_rows - sh, axis=0)
                for m in range(halo // SUBLANES + 1):
                    j = m * SUBLANES + sh - base
                    if 0 <= j < CONF_KERNEL:
                        acc = acc + cw_ref[j:j + 1, cs] * shifted[m * SUBLANES:m * SUBLANES + rblk]
            conv_sc[pl.ds(rb * rblk, rblk), cs] = acc
        return carry

    lax.fori_loop(0, width // cblk, col_block, 0)

    c = conv_sc[...]
    mu = jnp.mean(c, axis=-1, keepdims=True)
    cc = c - mu
    var = jnp.mean(cc * cc, axis=-1, keepdims=True)
    ln = cc * lax.rsqrt(var + NORM_EPS) * lnw_ref[...] + lnb_ref[...]
    act = (_silu(ln) * _silu(g_ref[0].astype(F32))).astype(BF16)
    h = h_ref[0] + jnp.dot(act, wo_ref[...], preferred_element_type=F32)
    h = _ple(h, p_ref.at[0], pw_ref, gw_ref)
    out_ref[0] = _rms_rows(h, fw_ref[...])


def _conv_entry(u, g, h, p, conv_w, conv_b, ln_w, ln_b, wo, pw, gw, fw):
    row = lambda v: v.reshape(1, -1).astype(F32)
    halo = CONV_HALO
    conv_w = jnp.pad(conv_w.astype(F32), ((0, halo - CONF_KERNEL), (0, 0)))
    return _conv(u, g, h, p, conv_w, row(conv_b), row(ln_w), row(ln_b), wo, pw, gw, row(fw),
                 tm=_pick(u.shape[1], CONV_TILE), halo=halo, cblk=LANES)


def _conv(u, g, h, p, conv_w, conv_b, ln_w, ln_b, wo, pw, gw, fw, *, tm, halo, cblk):
    b, s, cw = u.shape
    d = h.shape[-1]
    per = tm // halo
    blk = lambda n: pl.BlockSpec((1, tm, n), lambda bi, i: (bi, i, 0))
    halo_spec = pl.BlockSpec((1, halo, cw), lambda bi, i: (bi, jnp.maximum(i * per - 1, 0), 0))
    consts = [conv_w, conv_b, ln_w, ln_b, wo, pw, gw, fw]
    return pl.pallas_call(
        functools.partial(_conv_kernel, tm=tm, halo=halo, cblk=cblk),
        grid=(b, s // tm),
        in_specs=[blk(cw), halo_spec, blk(cw), blk(d), blk(p.shape[-1])]
                 + [_const_spec(c.shape) for c in consts],
        out_specs=blk(d),
        out_shape=jax.ShapeDtypeStruct((b, s, d), F32),
        scratch_shapes=[pltpu.VMEM((halo + tm, cw), F32), pltpu.VMEM((tm, cw), F32)],
        compiler_params=_params(("parallel", "parallel")),
        name="conv_tail",
    )(u, u, g, h, p, *consts)


def _rope_tables(seq):
    half = DIFF_HEAD_DIM // 2
    pos = jnp.arange(seq, dtype=F32)
    inv = ROPE_THETA ** (-jnp.arange(0, DIFF_HEAD_DIM, 2, dtype=F32) / DIFF_HEAD_DIM)
    ang = pos[:, None] * inv[None, :]
    cos, sin = jnp.cos(ang), jnp.sin(ang)
    reps = LANES // DIFF_HEAD_DIM
    cos_t = jnp.tile(jnp.concatenate([cos, cos], axis=1), (1, reps))
    sin_t = jnp.tile(jnp.concatenate([-sin, sin], axis=1), (1, reps))
    return cos_t, sin_t


def _pick(n, pref):
    while n % pref:
        pref //= 2
    return pref


def kernel(x, p, norm_w, ple_w, ple_gate_w, even_w_in, ssd_conv_w, ssd_conv_b, ssd_dt_bias, ssd_a_log, ssd_d, ssd_norm_w, diff_lambda, diff_subln_w, even_w_out, conf_w_in, conf_conv_w, conf_conv_b, conf_ln_w, conf_ln_b, conf_w_out, final_norm_w):
    b, s, d = x.shape
    t = b * s
    x2 = x.reshape(t, d)
    row = lambda v: v.reshape(1, -1).astype(F32)

    w_in = even_w_in[0].astype(BF16)
    o_z, o_xbc = SSD_WIDTH, SSD_WIDTH + SSD_CONV_CH
    o_dt = o_xbc + SSD_HEADS
    o_q, o_k, o_v = o_dt + DIFF_WIDTH, o_dt + 2 * DIFF_WIDTH, o_dt + 3 * DIFF_WIDTH
    w_dt = jnp.pad(w_in[:, o_xbc:o_dt], ((0, 0), (0, LANES - SSD_HEADS)))
    ws0 = (w_in[:, :o_z], w_in[:, o_z:o_xbc], w_dt, w_in[:, o_dt:o_q], w_in[:, o_q:o_k],
           w_in[:, o_k:o_v], w_in[:, o_v:])
    cos_t, sin_t = _rope_tables(s)
    tm0 = _pick(s, 256)
    z, xbc, dt, q, k, v, g = _proj0(x2, row(norm_w[0]), cos_t, sin_t, ws0, seq=s, tm=tm0)

    head_of_channel = jnp.arange(SSD_WIDTH) // SSD_HEAD_DIM
    expand = (jnp.arange(LANES)[:, None] == head_of_channel[None, :]).astype(F32)
    a_neg = -jnp.exp(ssd_a_log[0].astype(F32))
    pad_h = lambda vec: jnp.pad(vec.astype(F32), (0, LANES - SSD_HEADS)).reshape(1, LANES)
    y = _ssd(xbc.reshape(b, s, -1), dt.reshape(b, s, -1), z.reshape(b, s, -1),
             ssd_conv_w[0].astype(F32), row(ssd_conv_b[0]), pad_h(ssd_dt_bias[0]), pad_h(a_neg),
             row(a_neg[head_of_channel]), row(ssd_d[0][head_of_channel]), row(ssd_norm_w[0]), expand)

    o = _attn_entry(q.reshape(b, s, -1), k.reshape(b, s, -1), v.reshape(b, s, -1),
                    g.reshape(b, s, -1), diff_lambda[0], diff_subln_w[0])

    w_out = even_w_out[0].astype(BF16)
    h1 = _out0(y.reshape(t, -1), o.reshape(t, -1), x2, p[0].reshape(t, -1),
               w_out[:SSD_WIDTH], w_out[SSD_WIDTH:], ple_w[0].astype(BF16),
               ple_gate_w[0].astype(BF16), tm=_pick(t, 512))

    w1 = conf_w_in[0].astype(BF16)
    cw = w1.shape[1] // 3
    u, g1 = _proj1(h1, row(norm_w[1]), w1[:, :cw], w1[:, cw:2 * cw], w1[:, 2 * cw:], tm=_pick(t, 256))
    return _conv_entry(u.reshape(b, s, cw), g1.reshape(b, s, cw), h1.reshape(b, s, d), p[1],
                       conf_conv_w[0], conf_conv_b[0], conf_ln_w[0], conf_ln_b[0],
                       conf_w_out[0].astype(BF16), ple_w[1].astype(BF16),
                       ple_gate_w[1].astype(BF16), final_norm_w)
```

```python
import functools
import math

import jax
import jax.numpy as jnp
from jax import lax
from jax.experimental import pallas as pl
from jax.experimental.pallas import tpu as pltpu

F32 = jnp.float32
BF16 = jnp.bfloat16

NORM_EPS = 1e-6
ROPE_THETA = 10000.0

SSD_HEADS = 16
SSD_HEAD_DIM = 64
SSD_WIDTH = SSD_HEADS * SSD_HEAD_DIM
SSD_GROUPS = 2
SSD_STATE = 128
SSD_CONV = 4
SSD_CHUNK = 128
SSD_BC = SSD_GROUPS * SSD_STATE
SSD_CONV_CH = SSD_WIDTH + 2 * SSD_BC
SSD_GROUP_WIDTH = SSD_WIDTH // SSD_GROUPS
SSD_CHUNKS_PER_STEP = 2
SSD_TAIL = 16

DIFF_HEADS = 8
DIFF_HEAD_DIM = 64
DIFF_V_DIM = 2 * DIFF_HEAD_DIM
DIFF_WIDTH = DIFF_HEADS * DIFF_V_DIM

CONF_KERNEL = 31

LANES = 128
SUBLANES = 8
VMEM_LIMIT = 48 * 1024 * 1024
MASK_VALUE = -1e30
ATTN_TILE = 512
CONV_TILE = 256
CONV_HALO = 32
CONV_ROW_BLOCK = 128

HIGHEST = lax.Precision.HIGHEST


def _silu(x):
    return x * jax.nn.sigmoid(x)


def _rms_rows(x, w):
    ms = jnp.mean(x * x, axis=-1, keepdims=True)
    return x * lax.rsqrt(ms + NORM_EPS) * w


def _const_spec(shape):
    return pl.BlockSpec(shape, lambda *_: (0,) * len(shape))


def _params(sem):
    return pltpu.CompilerParams(dimension_semantics=sem, vmem_limit_bytes=VMEM_LIMIT)


def _proj0_kernel(x_ref, nw_ref, cos_ref, sin_ref, wz, wxbc, wdt, wq, wk, wv, wg,
                  z_o, xbc_o, dt_o, q_o, k_o, v_o, g_o, *, q_scale):
    hn = _rms_rows(x_ref[...], nw_ref[...]).astype(BF16)

    def mm(w_ref):
        return jnp.dot(hn, w_ref[...], preferred_element_type=F32)

    z_o[...] = mm(wz).astype(BF16)
    xbc_o[...] = mm(wxbc).astype(BF16)
    dt_o[...] = mm(wdt)
    v_o[...] = mm(wv).astype(BF16)
    g_o[...] = mm(wg).astype(BF16)

    width = q_o.shape[-1]
    reps = width // LANES
    cos = jnp.tile(cos_ref[...], (1, reps))
    sin = jnp.tile(sin_ref[...], (1, reps))
    lane = lax.broadcasted_iota(jnp.int32, (1, width), 1)
    first_half = (lane % DIFF_HEAD_DIM) < (DIFF_HEAD_DIM // 2)

    def rope(t):
        partner = jnp.where(first_half,
                            pltpu.roll(t, width - DIFF_HEAD_DIM // 2, axis=1),
                            pltpu.roll(t, DIFF_HEAD_DIM // 2, axis=1))
        return t * cos + partner * sin

    q_o[...] = (rope(mm(wq)) * q_scale).astype(BF16)
    k_o[...] = rope(mm(wk)).astype(BF16)


def _proj0(x2, norm_w, cos_t, sin_t, ws, *, seq, tm):
    t, d = x2.shape
    wz, wxbc, wdt, wq, wk, wv, wg = ws
    nblk_seq = seq // tm
    row = lambda n: pl.BlockSpec((tm, n), lambda i: (i, 0))
    pos = pl.BlockSpec((tm, LANES), lambda i: (i % nblk_seq, 0))
    outs = [(SSD_WIDTH, BF16), (SSD_CONV_CH, BF16), (LANES, F32), (DIFF_WIDTH, BF16),
            (DIFF_WIDTH, BF16), (DIFF_WIDTH, BF16), (DIFF_WIDTH, BF16)]
    return pl.pallas_call(
        functools.partial(_proj0_kernel, q_scale=DIFF_HEAD_DIM ** -0.5 * math.log2(math.e)),
        grid=(t // tm,),
        in_specs=[row(d), _const_spec((1, d)), pos, pos] + [_const_spec(w.shape) for w in ws],
        out_specs=[row(n) for n, _ in outs],
        out_shape=[jax.ShapeDtypeStruct((t, n), dt) for n, dt in outs],
        compiler_params=_params(("parallel",)),
        name="proj0",
    )(x2, norm_w, cos_t, sin_t, *ws)


def _ssd_kernel(xbc_ref, halo_ref, dt_ref, z_ref, cw_ref, cb_ref, dtb_ref, aneg_ref,
                dskip_ref, nw_ref, expand_ref, shift_ref, y_ref, state_sc, *, chunks):
    L = SSD_CHUNK
    first = pl.program_id(1) == 0

    @pl.when(first)
    def _():
        state_sc[...] = jnp.zeros_like(state_sc)

    prev_rows = halo_ref[0]
    prev_rows = jnp.where(first, jnp.zeros_like(prev_rows), prev_rows)

    for c in range(chunks):
        lo_row = c * L
        if c == 0:
            win = jnp.concatenate([prev_rows, xbc_ref[0, 0:L, :]], axis=0)
        else:
            win = xbc_ref[0, lo_row - SSD_TAIL:lo_row + L, :]
        y = _ssd_chunk(win, dt_ref[0, lo_row:lo_row + L, :], z_ref[0, lo_row:lo_row + L, :],
                       cw_ref, cb_ref, dtb_ref, aneg_ref, dskip_ref, nw_ref, expand_ref, shift_ref,
                       state_sc)
        y_ref[0, lo_row:lo_row + L, :] = y


def _ssd_chunk(win, dt_raw, z_bf, cw_ref, cb_ref, dtb_ref, aneg_ref, dskip_ref, nw_ref,
               expand_ref, shift_ref, state_sc):
    L = SSD_CHUNK
    taps = jnp.dot(shift_ref[...], win, preferred_element_type=F32)
    acc = cb_ref[...] + cw_ref[0:1, :] * taps[0:L]
    for j in range(1, SSD_CONV):
        acc = acc + cw_ref[j:j + 1, :] * taps[j * L:(j + 1) * L]
    xbc = _silu(acc)
    xs = xbc[:, :SSD_WIDTH]
    bm = xbc[:, SSD_WIDTH:SSD_WIDTH + SSD_BC].astype(BF16)
    cm = xbc[:, SSD_WIDTH + SSD_BC:].astype(BF16)

    dt = jax.nn.softplus(dt_raw + dtb_ref[...])
    a = dt * aneg_ref[...]
    r_i = lax.broadcasted_iota(jnp.int32, (L, L), 0)
    c_i = lax.broadcasted_iota(jnp.int32, (L, L), 1)
    causal = r_i >= c_i
    tril = causal.astype(F32)
    acs = jnp.dot(tril, a, precision=HIGHEST, preferred_element_type=F32)
    acs_t = acs.T
    both = jnp.concatenate([dt, acs], axis=0)
    hi = both.astype(BF16)
    rest = both - hi.astype(F32)
    mid = rest.astype(BF16)
    lo = (rest - mid.astype(F32)).astype(BF16)
    both_x = jnp.dot(jnp.concatenate([hi, mid, lo], axis=1), expand_ref[...],
                     preferred_element_type=F32)
    dt_x = both_x[:L]
    acs_x = both_x[L:]
    acs_last = acs_x[L - 1:L, :]

    xd = xs * dt_x
    xd_bf = xd.astype(BF16)
    xdd_bf = (xd * jnp.exp(acs_last - acs_x)).astype(BF16)
    grow = jnp.exp(acs_x)
    chunk_decay = jnp.exp(acs_last)

    nt = (((1,), (1,)), ((), ()))
    tn = (((0,), (0,)), ((), ()))
    heads_per_group = SSD_HEADS // SSD_GROUPS
    quad_w = 2 * LANES
    heads_per_quad = quad_w // SSD_HEAD_DIM
    lane_q = lax.broadcasted_iota(jnp.int32, (1, quad_w), 1) // SSD_HEAD_DIM

    y_parts = []
    for g in range(SSD_GROUPS):
        bg = bm[:, g * SSD_STATE:(g + 1) * SSD_STATE]
        cg = cm[:, g * SSD_STATE:(g + 1) * SSD_STATE]
        gs = slice(g * SSD_GROUP_WIDTH, (g + 1) * SSD_GROUP_WIDTH)
        cb = lax.dot_general(cg, bg, nt, preferred_element_type=F32)
        prev = state_sc[g]
        y_g = jnp.dot(cg, prev.astype(BF16), preferred_element_type=F32) * grow[:, gs]
        new = lax.dot_general(bg, xdd_bf[:, gs], tn, preferred_element_type=F32)
        state_sc[g] = prev * chunk_decay[:, gs] + new
        quads = []
        for qd in range(SSD_GROUP_WIDTH // quad_w):
            lo = g * SSD_GROUP_WIDTH + qd * quad_w
            x_q = xd_bf[:, lo:lo + quad_w]
            y_q = jnp.zeros((L, quad_w), F32)
            for hq in range(heads_per_quad):
                h = g * heads_per_group + qd * heads_per_quad + hq
                diff = jnp.broadcast_to(acs[:, h:h + 1], (L, L)) - acs_t[h:h + 1, :]
                lmat = jnp.where(causal, jnp.exp(jnp.where(causal, diff, 0.0)), 0.0)
                m = (cb * lmat).astype(BF16)
                r = jnp.dot(m, x_q, preferred_element_type=F32)
                y_q = jnp.where(lane_q == hq, r, y_q)
            quads.append(y_q)
        y_parts.append(y_g + jnp.concatenate(quads, axis=1))

    z = z_bf.astype(F32)
    outs = []
    for g in range(SSD_GROUPS):
        gs = slice(g * SSD_GROUP_WIDTH, (g + 1) * SSD_GROUP_WIDTH)
        yz = (y_parts[g] + xs[:, gs] * dskip_ref[:, gs]) * _silu(z[:, gs])
        outs.append(_rms_rows(yz, nw_ref[:, gs]))
    return jnp.concatenate(outs, axis=1).astype(BF16)


def _ssd_entry(xbc, dt, z, conv_w, conv_b, dt_bias, a_log, d_skip, norm_w):
    row = lambda v: v.reshape(1, -1).astype(F32)
    pad_h = lambda vec: jnp.pad(vec.astype(F32), (0, LANES - SSD_HEADS)).reshape(1, LANES)
    head_of_channel = jnp.arange(SSD_WIDTH) // SSD_HEAD_DIM
    expand = (jnp.arange(LANES)[:, None] == head_of_channel[None, :]).astype(BF16)
    a_neg = -jnp.exp(a_log.astype(F32))
    return _ssd(xbc, dt, z, conv_w.astype(F32), row(conv_b), pad_h(dt_bias), pad_h(a_neg),
                row(d_skip[head_of_channel]), row(norm_w), jnp.tile(expand, (3, 1)),
                _shift_matrix(SSD_CONV, SSD_CHUNK, SSD_TAIL))


def _shift_matrix(taps, rows, tail):
    out_row = jnp.arange(taps * rows)
    src = out_row % rows + tail - (taps - 1 - out_row // rows)
    return (src[:, None] == jnp.arange(tail + rows)[None, :]).astype(BF16)


def _ssd(xbc, dt, z, conv_w, conv_b, dt_bias, a_neg, d_skip_x, norm_w, expand, shift):
    b, s, _ = xbc.shape
    chunks = _pick(s // SSD_CHUNK, SSD_CHUNKS_PER_STEP)
    rows = chunks * SSD_CHUNK
    blk = lambda n: pl.BlockSpec((1, rows, n), lambda bi, ci: (bi, ci, 0))
    consts = [conv_w, conv_b, dt_bias, a_neg, d_skip_x, norm_w, expand, shift]
    per = rows // SSD_TAIL
    halo = pl.BlockSpec((1, SSD_TAIL, SSD_CONV_CH),
                        lambda bi, ci: (bi, jnp.maximum(ci * per - 1, 0), 0))
    return pl.pallas_call(
        functools.partial(_ssd_kernel, chunks=chunks),
        grid=(b, s // rows),
        in_specs=[blk(SSD_CONV_CH), halo, blk(LANES), blk(SSD_WIDTH)]
                 + [_const_spec(c.shape) for c in consts],
        out_specs=blk(SSD_WIDTH),
        out_shape=jax.ShapeDtypeStruct((b, s, SSD_WIDTH), BF16),
        scratch_shapes=[pltpu.VMEM((SSD_GROUPS, SSD_STATE, SSD_GROUP_WIDTH), F32)],
        compiler_params=_params(("parallel", "arbitrary")),
        name="ssd",
    )(xbc, xbc, dt, z, *consts)


def _attn_kernel(lam_ref, q_ref, k_ref, v_ref, g_ref, sw_ref, o_ref,
                 qm_sc, sa_sc, sb_sc, mxa_sc, mxb_sc, m_sc, l_sc, acc_sc, *, tq, lambda_init):
    qi = pl.program_id(2)
    nt = (((1,), (1,)), ((), ()))
    tn = (((0,), (0,)), ((), ()))

    q = q_ref[0]
    lane = lax.broadcasted_iota(jnp.int32, q.shape, 1)
    zero = jnp.zeros_like(q)
    qm_sc[0] = jnp.where(lane < DIFF_HEAD_DIM, q, zero)
    qm_sc[1] = jnp.where(lane >= DIFF_HEAD_DIM, q, zero)

    m_sc[...] = jnp.full(m_sc.shape, MASK_VALUE, F32)
    l_sc[...] = jnp.zeros(l_sc.shape, F32)
    acc_sc[...] = jnp.zeros(acc_sc.shape, F32)

    def scores(j, masked, s_ref, mx_ref):
        start = pl.multiple_of(j * tq, tq)
        k = k_ref[0, pl.ds(start, tq), :]
        for c in range(2):
            s = lax.dot_general(k, qm_sc[c], nt, preferred_element_type=F32)
            if masked:
                kpos = lax.broadcasted_iota(jnp.int32, s.shape, 0)
                qpos = lax.broadcasted_iota(jnp.int32, s.shape, 1)
                s = jnp.where(kpos <= qpos, s, MASK_VALUE)
            s_ref[c] = s
            mx_ref[c] = jnp.max(s, axis=0, keepdims=True)

    def consume(j, s_ref, mx_ref):
        start = pl.multiple_of(j * tq, tq)
        v = v_ref[0, pl.ds(start, tq), :]
        for c in range(2):
            m_old = m_sc[c]
            m_new = jnp.maximum(m_old, mx_ref[c])
            p = jnp.exp2(s_ref[c] - m_new)
            alpha = jnp.exp2(m_old - m_new)
            l_sc[c] = alpha * l_sc[c] + jnp.sum(p, axis=0, keepdims=True)
            pv = lax.dot_general(v, p.astype(BF16), tn, preferred_element_type=F32)
            acc_sc[c] = alpha * acc_sc[c] + pv
            m_sc[c] = m_new

    scores(qi, True, sa_sc, mxa_sc)

    def pair(i, carry):
        j = 2 * i
        scores(j, False, sb_sc, mxb_sc)
        consume(jnp.where(i == 0, qi, j - 1), sa_sc, mxa_sc)
        scores(j + 1, False, sa_sc, mxa_sc)
        consume(j, sb_sc, mxb_sc)
        return carry

    npairs = qi // 2
    lax.fori_loop(0, npairs, pair, 0)
    pending = jnp.where(npairs == 0, qi, 2 * npairs - 1)

    @pl.when(qi % 2 == 0)
    def _():
        consume(pending, sa_sc, mxa_sc)

    @pl.when(qi % 2 == 1)
    def _():
        scores(qi - 1, False, sb_sc, mxb_sc)
        consume(pending, sa_sc, mxa_sc)
        consume(qi - 1, sb_sc, mxb_sc)


    lv = lam_ref[...]
    lam = (jnp.exp(jnp.sum(lv[0:1] * lv[1:2], axis=1, keepdims=True))
           - jnp.exp(jnp.sum(lv[2:3] * lv[3:4], axis=1, keepdims=True)) + lambda_init)
    o_t = acc_sc[0] / l_sc[0] - lam * (acc_sc[1] / l_sc[1])
    o = _rms_rows(o_t.T, sw_ref[...]) * (1.0 - lambda_init)
    o_ref[0] = (o * _silu(g_ref[0].astype(F32))).astype(BF16)


def _attn_entry(q, k, v, g, lam_vecs, subln_w):
    lambda_init = 0.8 - 0.6 * math.exp(-0.3 * 0)
    return _attn(q, k, v, g, lam_vecs.astype(F32), subln_w.reshape(1, -1).astype(F32),
                 lambda_init=lambda_init, tq=_pick(q.shape[1], ATTN_TILE))


def _attn(q, k, v, g, lam_vecs, subln_w, *, lambda_init, tq):
    b, s, _ = q.shape
    qblk = pl.BlockSpec((1, tq, DIFF_V_DIM), lambda bi, hi, qi: (bi, qi, hi))
    kvblk = pl.BlockSpec((1, s, DIFF_V_DIM), lambda bi, hi, qi: (bi, 0, hi))
    stat = pltpu.VMEM((2, 1, tq), F32)
    sbuf = pltpu.VMEM((2, tq, tq), F32)
    return pl.pallas_call(
        functools.partial(_attn_kernel, tq=tq, lambda_init=lambda_init),
        grid=(b, DIFF_HEADS, s // tq),
        in_specs=[_const_spec(lam_vecs.shape), qblk, kvblk, kvblk, qblk, _const_spec(subln_w.shape)],
        out_specs=qblk,
        out_shape=jax.ShapeDtypeStruct((b, s, DIFF_WIDTH), BF16),
        scratch_shapes=[pltpu.VMEM((2, tq, DIFF_V_DIM), BF16), sbuf, sbuf, stat, stat, stat, stat,
                        pltpu.VMEM((2, DIFF_V_DIM, tq), F32)],
        compiler_params=_params(("parallel", "parallel", "arbitrary")),
        name="diff_attn",
    )(lam_vecs, q, k, v, g, subln_w)


def _ple(h, p_ref, pw_ref, gw_ref):
    e = jnp.dot(p_ref[...].astype(BF16), pw_ref[...], preferred_element_type=F32)
    gate = jax.nn.sigmoid(jnp.dot(h.astype(BF16), gw_ref[...], preferred_element_type=F32))
    return h + e * gate


def _out0_kernel(y_ref, o_ref, x_ref, p_ref, wy_ref, wo_ref, pw_ref, gw_ref, h_ref):
    mix = (jnp.dot(y_ref[...], wy_ref[...], preferred_element_type=F32)
           + jnp.dot(o_ref[...], wo_ref[...], preferred_element_type=F32))
    h_ref[...] = _ple(x_ref[...] + mix, p_ref, pw_ref, gw_ref)


def _out0(y2, o2, x2, p2, wy, wo, pw, gw, *, tm):
    t, d = x2.shape
    row = lambda n: pl.BlockSpec((tm, n), lambda i: (i, 0))
    ws = [wy, wo, pw, gw]
    return pl.pallas_call(
        _out0_kernel,
        grid=(t // tm,),
        in_specs=[row(y2.shape[1]), row(o2.shape[1]), row(d), row(p2.shape[1])]
                 + [_const_spec(w.shape) for w in ws],
        out_specs=row(d),
        out_shape=jax.ShapeDtypeStruct((t, d), F32),
        compiler_params=_params(("parallel",)),
        name="out0",
    )(y2, o2, x2, p2, *ws)


def _proj1_kernel(h_ref, nw_ref, wu, wug, wg, u_o, g_o):
    hn = _rms_rows(h_ref[...], nw_ref[...]).astype(BF16)
    u = jnp.dot(hn, wu[...], preferred_element_type=F32)
    ug = jnp.dot(hn, wug[...], preferred_element_type=F32)
    u_o[...] = (u * jax.nn.sigmoid(ug)).astype(BF16)
    g_o[...] = jnp.dot(hn, wg[...], preferred_element_type=F32).astype(BF16)


def _proj1(h2, norm_w, wu, wug, wg, *, tm):
    t, d = h2.shape
    cw = wu.shape[1]
    row = lambda n: pl.BlockSpec((tm, n), lambda i: (i, 0))
    ws = [wu, wug, wg]
    return pl.pallas_call(
        _proj1_kernel,
        grid=(t // tm,),
        in_specs=[row(d), _const_spec((1, d))] + [_const_spec(w.shape) for w in ws],
        out_specs=[row(cw), row(cw)],
        out_shape=[jax.ShapeDtypeStruct((t, cw), BF16)] * 2,
        compiler_params=_params(("parallel",)),
        name="proj1",
    )(h2, norm_w, *ws)


def _conv_kernel(u_ref, halo_ref, g_ref, h_ref, p_ref, cw_ref, cb_ref, lnw_ref, lnb_ref,
                 wo_ref, pw_ref, gw_ref, fw_ref, out_ref, ubuf_sc, conv_sc, *, tm, halo, cblk):
    width = u_ref.shape[-1]
    first = pl.program_id(1) == 0
    halo_rows = halo_ref[0].astype(F32)
    ubuf_sc[0:halo, :] = jnp.where(first, jnp.zeros_like(halo_rows), halo_rows)
    ubuf_sc[halo:halo + tm, :] = u_ref[0].astype(F32)

    base = halo - (CONF_KERNEL - 1)
    rblk = min(tm, CONV_ROW_BLOCK)
    win_rows = rblk + halo

    def col_block(cb, carry):
        cs = pl.ds(pl.multiple_of(cb * cblk, cblk), cblk)
        for rb in range(tm // rblk):
            win = ubuf_sc[pl.ds(rb * rblk, win_rows), cs]
            acc = jnp.broadcast_to(cb_ref[:, cs], (rblk, cblk))
            for sh in range(SUBLANES):
                shifted = win if sh == 0 else pltpu.roll(win, win_rows - sh, axis=0)
                for m in range(halo // SUBLANES + 1):
                    j = m * SUBLANES + sh - base
                    if 0 <= j < CONF_KERNEL:
                        acc = acc + cw_ref[j:j + 1, cs] * shifted[m * SUBLANES:m * SUBLANES + rblk]
            conv_sc[pl.ds(rb * rblk, rblk), cs] = acc
        return carry

    lax.fori_loop(0, width // cblk, col_block, 0)

    c = conv_sc[...]
    mu = jnp.mean(c, axis=-1, keepdims=True)
    cc = c - mu
    var = jnp.mean(cc * cc, axis=-1, keepdims=True)
    ln = cc * lax.rsqrt(var + NORM_EPS) * lnw_ref[...] + lnb_ref[...]
    act = (_silu(ln) * _silu(g_ref[0].astype(F32))).astype(BF16)
    h = h_ref[0] + jnp.dot(act, wo_ref[...], preferred_element_type=F32)
    h = _ple(h, p_ref.at[0], pw_ref, gw_ref)
    out_ref[0] = _rms_rows(h, fw_ref[...])


def _conv_entry(u, g, h, p, conv_w, conv_b, ln_w, ln_b, wo, pw, gw, fw):
    row = lambda v: v.reshape(1, -1).astype(F32)
    halo = CONV_HALO
    conv_w = jnp.pad(conv_w.astype(F32), ((0, halo - CONF_KERNEL), (0, 0)))
    return _conv(u, g, h, p, conv_w, row(conv_b), row(ln_w), row(ln_b), wo, pw, gw, row(fw),
                 tm=_pick(u.shape[1], CONV_TILE), halo=halo, cblk=LANES)


def _conv(u, g, h, p, conv_w, conv_b, ln_w, ln_b, wo, pw, gw, fw, *, tm, halo, cblk):
    b, s, cw = u.shape
    d = h.shape[-1]
    per = tm // halo
    blk = lambda n: pl.BlockSpec((1, tm, n), lambda bi, i: (bi, i, 0))
    halo_spec = pl.BlockSpec((1, halo, cw), lambda bi, i: (bi, jnp.maximum(i * per - 1, 0), 0))
    consts = [conv_w, conv_b, ln_w, ln_b, wo, pw, gw, fw]
    return pl.pallas_call(
        functools.partial(_conv_kernel, tm=tm, halo=halo, cblk=cblk),
        grid=(b, s // tm),
        in_specs=[blk(cw), halo_spec, blk(cw), blk(d), blk(p.shape[-1])]
                 + [_const_spec(c.shape) for c in consts],
        out_specs=blk(d),
        out_shape=jax.ShapeDtypeStruct((b, s, d), F32),
        scratch_shapes=[pltpu.VMEM((halo + tm, cw), F32), pltpu.VMEM((tm, cw), F32)],
        compiler_params=_params(("parallel", "parallel")),
        name="conv_tail",
    )(u, u, g, h, p, *consts)


def _rope_tables(seq):
    half = DIFF_HEAD_DIM // 2
    pos = jnp.arange(seq, dtype=F32)
    inv = ROPE_THETA ** (-jnp.arange(0, DIFF_HEAD_DIM, 2, dtype=F32) / DIFF_HEAD_DIM)
    ang = pos[:, None] * inv[None, :]
    cos, sin = jnp.cos(ang), jnp.sin(ang)
    reps = LANES // DIFF_HEAD_DIM
    cos_t = jnp.tile(jnp.concatenate([cos, cos], axis=1), (1, reps))
    sin_t = jnp.tile(jnp.concatenate([-sin, sin], axis=1), (1, reps))
    return cos_t, sin_t


def _pick(n, pref):
    while n % pref:
        pref //= 2
    return pref


def kernel(x, p, norm_w, ple_w, ple_gate_w, even_w_in, ssd_conv_w, ssd_conv_b, ssd_dt_bias, ssd_a_log, ssd_d, ssd_norm_w, diff_lambda, diff_subln_w, even_w_out, conf_w_in, conf_conv_w, conf_conv_b, conf_ln_w, conf_ln_b, conf_w_out, final_norm_w):
    b, s, d = x.shape
    t = b * s
    x2 = x.reshape(t, d)
    row = lambda v: v.reshape(1, -1).astype(F32)

    w_in = even_w_in[0].astype(BF16)
    o_z, o_xbc = SSD_WIDTH, SSD_WIDTH + SSD_CONV_CH
    o_dt = o_xbc + SSD_HEADS
    o_q, o_k, o_v = o_dt + DIFF_WIDTH, o_dt + 2 * DIFF_WIDTH, o_dt + 3 * DIFF_WIDTH
    w_dt = jnp.pad(w_in[:, o_xbc:o_dt], ((0, 0), (0, LANES - SSD_HEADS)))
    ws0 = (w_in[:, :o_z], w_in[:, o_z:o_xbc], w_dt, w_in[:, o_dt:o_q], w_in[:, o_q:o_k],
           w_in[:, o_k:o_v], w_in[:, o_v:])
    cos_t, sin_t = _rope_tables(s)
    tm0 = _pick(s, 256)
    z, xbc, dt, q, k, v, g = _proj0(x2, row(norm_w[0]), cos_t, sin_t, ws0, seq=s, tm=tm0)

    y = _ssd_entry(xbc.reshape(b, s, -1), dt.reshape(b, s, -1), z.reshape(b, s, -1),
                   ssd_conv_w[0], ssd_conv_b[0], ssd_dt_bias[0], ssd_a_log[0], ssd_d[0],
                   ssd_norm_w[0])

    o = _attn_entry(q.reshape(b, s, -1), k.reshape(b, s, -1), v.reshape(b, s, -1),
                    g.reshape(b, s, -1), diff_lambda[0], diff_subln_w[0])

    w_out = even_w_out[0].astype(BF16)
    h1 = _out0(y.reshape(t, -1), o.reshape(t, -1), x2, p[0].reshape(t, -1),
               w_out[:SSD_WIDTH], w_out[SSD_WIDTH:], ple_w[0].astype(BF16),
               ple_gate_w[0].astype(BF16), tm=_pick(t, 512))

    w1 = conf_w_in[0].astype(BF16)
    cw = w1.shape[1] // 3
    u, g1 = _proj1(h1, row(norm_w[1]), w1[:, :cw], w1[:, cw:2 * cw], w1[:, 2 * cw:], tm=_pick(t, 256))
    return _conv_entry(u.reshape(b, s, cw), g1.reshape(b, s, cw), h1.reshape(b, s, d), p[1],
                       conf_conv_w[0], conf_conv_b[0], conf_ln_w[0], conf_ln_b[0],
                       conf_w_out[0].astype(BF16), ple_w[1].astype(BF16),
                       ple_gate_w[1].astype(BF16), final_norm_w)
```

```python
import functools
import math

import jax
import jax.numpy as jnp
from jax import lax
from jax.experimental import pallas as pl
from jax.experimental.pallas import tpu as pltpu

F32 = jnp.float32
BF16 = jnp.bfloat16

NORM_EPS = 1e-6
ROPE_THETA = 10000.0

SSD_HEADS = 16
SSD_HEAD_DIM = 64
SSD_WIDTH = SSD_HEADS * SSD_HEAD_DIM
SSD_GROUPS = 2
SSD_STATE = 128
SSD_CONV = 4
SSD_CHUNK = 128
SSD_BC = SSD_GROUPS * SSD_STATE
SSD_CONV_CH = SSD_WIDTH + 2 * SSD_BC
SSD_GROUP_WIDTH = SSD_WIDTH // SSD_GROUPS
SSD_CHUNKS_PER_STEP = 2
SSD_TAIL = 16

DIFF_HEADS = 8
DIFF_HEAD_DIM = 64
DIFF_V_DIM = 2 * DIFF_HEAD_DIM
DIFF_WIDTH = DIFF_HEADS * DIFF_V_DIM

CONF_KERNEL = 31

LANES = 128
SUBLANES = 8
VMEM_LIMIT = 48 * 1024 * 1024
MASK_VALUE = -1e30
ATTN_TILE = 1024
PROJ0_TILE = 512
PROJ1_TILE = 256
OUT0_TILE = 512
CONV_TILE = 256
CONV_HALO = 32
CONV_ROW_BLOCK = 128

HIGHEST = lax.Precision.HIGHEST


def _silu(x):
    return x * jax.nn.sigmoid(x)


def _rms_rows(x, w):
    ms = jnp.mean(x * x, axis=-1, keepdims=True)
    return x * lax.rsqrt(ms + NORM_EPS) * w


def _const_spec(shape):
    return pl.BlockSpec(shape, lambda *_: (0,) * len(shape), pipeline_mode=pl.Buffered(1))


def _params(sem):
    return pltpu.CompilerParams(dimension_semantics=sem, vmem_limit_bytes=VMEM_LIMIT)


def _proj0_kernel(x_ref, nw_ref, cos_ref, sin_ref, wz, wxbc, wdt, wq, wk, wv, wg,
                  z_o, xbc_o, dt_o, q_o, k_o, v_o, g_o, *, q_scale):
    hn = _rms_rows(x_ref[...], nw_ref[...]).astype(BF16)

    def mm(w_ref):
        return jnp.dot(hn, w_ref[...], preferred_element_type=F32)

    z_o[...] = mm(wz).astype(BF16)
    xbc_o[...] = mm(wxbc).astype(BF16)
    dt_o[...] = mm(wdt)
    v_o[...] = mm(wv).astype(BF16)
    g_o[...] = mm(wg).astype(BF16)

    width = q_o.shape[-1]
    reps = width // LANES
    cos = jnp.tile(cos_ref[...], (1, reps))
    sin = jnp.tile(sin_ref[...], (1, reps))
    lane = lax.broadcasted_iota(jnp.int32, (1, width), 1)
    first_half = (lane % DIFF_HEAD_DIM) < (DIFF_HEAD_DIM // 2)

    def rope(t):
        partner = jnp.where(first_half,
                            pltpu.roll(t, width - DIFF_HEAD_DIM // 2, axis=1),
                            pltpu.roll(t, DIFF_HEAD_DIM // 2, axis=1))
        return t * cos + partner * sin

    q_o[...] = (rope(mm(wq)) * q_scale).astype(BF16)
    k_o[...] = rope(mm(wk)).astype(BF16)


def _proj0(x2, norm_w, cos_t, sin_t, ws, *, seq, tm):
    t, d = x2.shape
    wz, wxbc, wdt, wq, wk, wv, wg = ws
    nblk_seq = seq // tm
    row = lambda n: pl.BlockSpec((tm, n), lambda i: (i, 0))
    pos = pl.BlockSpec((tm, LANES), lambda i: (i % nblk_seq, 0))
    outs = [(SSD_WIDTH, BF16), (SSD_CONV_CH, BF16), (LANES, F32), (DIFF_WIDTH, BF16),
            (DIFF_WIDTH, BF16), (DIFF_WIDTH, BF16), (DIFF_WIDTH, BF16)]
    return pl.pallas_call(
        functools.partial(_proj0_kernel, q_scale=DIFF_HEAD_DIM ** -0.5 * math.log2(math.e)),
        grid=(t // tm,),
        in_specs=[row(d), _const_spec((1, d)), pos, pos] + [_const_spec(w.shape) for w in ws],
        out_specs=[row(n) for n, _ in outs],
        out_shape=[jax.ShapeDtypeStruct((t, n), dt) for n, dt in outs],
        compiler_params=_params(("parallel",)),
        name="proj0",
    )(x2, norm_w, cos_t, sin_t, *ws)


def _ssd_kernel(xbc_ref, halo_ref, dt_ref, z_ref, cw_ref, cb_ref, dtb_ref, aneg_ref,
                dskip_ref, nw_ref, expand_ref, shift_ref, y_ref, state_sc, *, chunks):
    L = SSD_CHUNK
    first = pl.program_id(1) == 0

    @pl.when(first)
    def _():
        state_sc[...] = jnp.zeros_like(state_sc)

    prev_rows = halo_ref[0]
    prev_rows = jnp.where(first, jnp.zeros_like(prev_rows), prev_rows)

    for c in range(chunks):
        lo_row = c * L
        if c == 0:
            win = jnp.concatenate([prev_rows, xbc_ref[0, 0:L, :]], axis=0)
        else:
            win = xbc_ref[0, lo_row - SSD_TAIL:lo_row + L, :]
        y = _ssd_chunk(win, dt_ref[0, lo_row:lo_row + L, :], z_ref[0, lo_row:lo_row + L, :],
                       cw_ref, cb_ref, dtb_ref, aneg_ref, dskip_ref, nw_ref, expand_ref, shift_ref,
                       state_sc)
        y_ref[0, lo_row:lo_row + L, :] = y


def _ssd_chunk(win, dt_raw, z_bf, cw_ref, cb_ref, dtb_ref, aneg_ref, dskip_ref, nw_ref,
               expand_ref, shift_ref, state_sc):
    L = SSD_CHUNK
    taps = jnp.dot(shift_ref[...], win, preferred_element_type=F32)
    acc = cb_ref[...] + cw_ref[0:1, :] * taps[0:L]
    for j in range(1, SSD_CONV):
        acc = acc + cw_ref[j:j + 1, :] * taps[j * L:(j + 1) * L]
    xbc = _silu(acc)
    xs = xbc[:, :SSD_WIDTH]
    bm = xbc[:, SSD_WIDTH:SSD_WIDTH + SSD_BC].astype(BF16)
    cm = xbc[:, SSD_WIDTH + SSD_BC:].astype(BF16)

    dt = jax.nn.softplus(dt_raw + dtb_ref[...])
    a = dt * aneg_ref[...]
    r_i = lax.broadcasted_iota(jnp.int32, (L, L), 0)
    c_i = lax.broadcasted_iota(jnp.int32, (L, L), 1)
    causal = r_i >= c_i
    tril = causal.astype(F32)
    acs = jnp.dot(tril, a, precision=HIGHEST, preferred_element_type=F32)
    acs_t = acs.T
    both = jnp.concatenate([dt, acs], axis=0)
    hi = both.astype(BF16)
    rest = both - hi.astype(F32)
    mid = rest.astype(BF16)
    lo = (rest - mid.astype(F32)).astype(BF16)
    both_x = jnp.dot(jnp.concatenate([hi, mid, lo], axis=1), expand_ref[...],
                     preferred_element_type=F32)
    dt_x = both_x[:L]
    acs_x = both_x[L:]
    acs_last = acs_x[L - 1:L, :]

    xd = xs * dt_x
    xd_bf = xd.astype(BF16)
    xdd_bf = (xd * jnp.exp(acs_last - acs_x)).astype(BF16)
    grow = jnp.exp(acs_x)
    chunk_decay = jnp.exp(acs_last)

    nt = (((1,), (1,)), ((), ()))
    tn = (((0,), (0,)), ((), ()))
    heads_per_group = SSD_HEADS // SSD_GROUPS
    quad_w = 2 * LANES
    heads_per_quad = quad_w // SSD_HEAD_DIM
    lane_q = lax.broadcasted_iota(jnp.int32, (1, quad_w), 1) // SSD_HEAD_DIM

    y_parts = []
    for g in range(SSD_GROUPS):
        bg = bm[:, g * SSD_STATE:(g + 1) * SSD_STATE]
        cg = cm[:, g * SSD_STATE:(g + 1) * SSD_STATE]
        gs = slice(g * SSD_GROUP_WIDTH, (g + 1) * SSD_GROUP_WIDTH)
        cb = lax.dot_general(cg, bg, nt, preferred_element_type=F32)
        prev = state_sc[g]
        y_g = jnp.dot(cg, prev.astype(BF16), preferred_element_type=F32) * grow[:, gs]
        new = lax.dot_general(bg, xdd_bf[:, gs], tn, preferred_element_type=F32)
        state_sc[g] = prev * chunk_decay[:, gs] + new
        quads = []
        for qd in range(SSD_GROUP_WIDTH // quad_w):
            lo = g * SSD_GROUP_WIDTH + qd * quad_w
            x_q = xd_bf[:, lo:lo + quad_w]
            y_q = jnp.zeros((L, quad_w), F32)
            for hq in range(heads_per_quad):
                h = g * heads_per_group + qd * heads_per_quad + hq
                diff = jnp.broadcast_to(acs[:, h:h + 1], (L, L)) - acs_t[h:h + 1, :]
                lmat = jnp.where(causal, jnp.exp(jnp.where(causal, diff, 0.0)), 0.0)
                m = (cb * lmat).astype(BF16)
                r = jnp.dot(m, x_q, preferred_element_type=F32)
                y_q = jnp.where(lane_q == hq, r, y_q)
            quads.append(y_q)
        y_parts.append(y_g + jnp.concatenate(quads, axis=1))

    z = z_bf.astype(F32)
    outs = []
    for g in range(SSD_GROUPS):
        gs = slice(g * SSD_GROUP_WIDTH, (g + 1) * SSD_GROUP_WIDTH)
        yz = (y_parts[g] + xs[:, gs] * dskip_ref[:, gs]) * _silu(z[:, gs])
        outs.append(_rms_rows(yz, nw_ref[:, gs]))
    return jnp.concatenate(outs, axis=1).astype(BF16)


def _ssd_entry(xbc, dt, z, conv_w, conv_b, dt_bias, a_log, d_skip, norm_w):
    row = lambda v: v.reshape(1, -1).astype(F32)
    pad_h = lambda vec: jnp.pad(vec.astype(F32), (0, LANES - SSD_HEADS)).reshape(1, LANES)
    head_of_channel = jnp.arange(SSD_WIDTH) // SSD_HEAD_DIM
    expand = (jnp.arange(LANES)[:, None] == head_of_channel[None, :]).astype(BF16)
    a_neg = -jnp.exp(a_log.astype(F32))
    return _ssd(xbc, dt, z, conv_w.astype(F32), row(conv_b), pad_h(dt_bias), pad_h(a_neg),
                row(d_skip[head_of_channel]), row(norm_w), jnp.tile(expand, (3, 1)),
                _shift_matrix(SSD_CONV, SSD_CHUNK, SSD_TAIL))


def _shift_matrix(taps, rows, tail):
    out_row = jnp.arange(taps * rows)
    src = out_row % rows + tail - (taps - 1 - out_row // rows)
    return (src[:, None] == jnp.arange(tail + rows)[None, :]).astype(BF16)


def _ssd(xbc, dt, z, conv_w, conv_b, dt_bias, a_neg, d_skip_x, norm_w, expand, shift):
    b, s, _ = xbc.shape
    chunks = _pick(s // SSD_CHUNK, SSD_CHUNKS_PER_STEP)
    rows = chunks * SSD_CHUNK
    blk = lambda n: pl.BlockSpec((1, rows, n), lambda bi, ci: (bi, ci, 0))
    consts = [conv_w, conv_b, dt_bias, a_neg, d_skip_x, norm_w, expand, shift]
    per = rows // SSD_TAIL
    halo = pl.BlockSpec((1, SSD_TAIL, SSD_CONV_CH),
                        lambda bi, ci: (bi, jnp.maximum(ci * per - 1, 0), 0))
    return pl.pallas_call(
        functools.partial(_ssd_kernel, chunks=chunks),
        grid=(b, s // rows),
        in_specs=[blk(SSD_CONV_CH), halo, blk(LANES), blk(SSD_WIDTH)]
                 + [_const_spec(c.shape) for c in consts],
        out_specs=blk(SSD_WIDTH),
        out_shape=jax.ShapeDtypeStruct((b, s, SSD_WIDTH), BF16),
        scratch_shapes=[pltpu.VMEM((SSD_GROUPS, SSD_STATE, SSD_GROUP_WIDTH), F32)],
        compiler_params=_params(("parallel", "arbitrary")),
        name="ssd",
    )(xbc, xbc, dt, z, *consts)


def _attn_kernel(lam_ref, q_ref, k_ref, v_ref, g_ref, sw_ref, o_ref,
                 qm_sc, sa_sc, sb_sc, mxa_sc, mxb_sc, m_sc, l_sc, acc_sc, *, tq, lambda_init):
    qi = pl.program_id(2)
    nt = (((1,), (1,)), ((), ()))
    tn = (((0,), (0,)), ((), ()))

    q = q_ref[0]
    lane = lax.broadcasted_iota(jnp.int32, q.shape, 1)
    zero = jnp.zeros_like(q)
    qm_sc[0] = jnp.where(lane < DIFF_HEAD_DIM, q, zero)
    qm_sc[1] = jnp.where(lane >= DIFF_HEAD_DIM, q, zero)

    m_sc[...] = jnp.full(m_sc.shape, MASK_VALUE, F32)
    l_sc[...] = jnp.zeros(l_sc.shape, F32)
    acc_sc[...] = jnp.zeros(acc_sc.shape, F32)

    def scores(j, masked, s_ref, mx_ref):
        start = pl.multiple_of(j * tq, tq)
        k = k_ref[0, pl.ds(start, tq), :]
        for c in range(2):
            s = lax.dot_general(k, qm_sc[c], nt, preferred_element_type=F32)
            if masked:
                kpos = lax.broadcasted_iota(jnp.int32, s.shape, 0)
                qpos = lax.broadcasted_iota(jnp.int32, s.shape, 1)
                s = jnp.where(kpos <= qpos, s, MASK_VALUE)
            s_ref[c] = s
            mx_ref[c] = jnp.max(s, axis=0, keepdims=True)

    def consume(j, s_ref, mx_ref):
        start = pl.multiple_of(j * tq, tq)
        v = v_ref[0, pl.ds(start, tq), :]
        for c in range(2):
            m_old = m_sc[c]
            m_new = jnp.maximum(m_old, mx_ref[c])
            p = jnp.exp2(s_ref[c] - m_new)
            alpha = jnp.exp2(m_old - m_new)
            l_sc[c] = alpha * l_sc[c] + jnp.sum(p, axis=0, keepdims=True)
            pv = lax.dot_general(v, p.astype(BF16), tn, preferred_element_type=F32)
            acc_sc[c] = alpha * acc_sc[c] + pv
            m_sc[c] = m_new

    scores(qi, True, sa_sc, mxa_sc)

    def pair(i, carry):
        j = 2 * i
        scores(j, False, sb_sc, mxb_sc)
        consume(jnp.where(i == 0, qi, j - 1), sa_sc, mxa_sc)
        scores(j + 1, False, sa_sc, mxa_sc)
        consume(j, sb_sc, mxb_sc)
        return carry

    npairs = qi // 2
    lax.fori_loop(0, npairs, pair, 0)
    pending = jnp.where(npairs == 0, qi, 2 * npairs - 1)

    @pl.when(qi % 2 == 0)
    def _():
        consume(pending, sa_sc, mxa_sc)

    @pl.when(qi % 2 == 1)
    def _():
        scores(qi - 1, False, sb_sc, mxb_sc)
        consume(pending, sa_sc, mxa_sc)
        consume(qi - 1, sb_sc, mxb_sc)


    lv = lam_ref[...]
    lam = (jnp.exp(jnp.sum(lv[0:1] * lv[1:2], axis=1, keepdims=True))
           - jnp.exp(jnp.sum(lv[2:3] * lv[3:4], axis=1, keepdims=True)) + lambda_init)
    o_t = acc_sc[0] / l_sc[0] - lam * (acc_sc[1] / l_sc[1])
    o = _rms_rows(o_t.T, sw_ref[...]) * (1.0 - lambda_init)
    o_ref[0] = (o * _silu(g_ref[0].astype(F32))).astype(BF16)


def _attn_entry(q, k, v, g, lam_vecs, subln_w):
    lambda_init = 0.8 - 0.6 * math.exp(-0.3 * 0)
    return _attn(q, k, v, g, lam_vecs.astype(F32), subln_w.reshape(1, -1).astype(F32),
                 lambda_init=lambda_init, tq=_pick(q.shape[1], ATTN_TILE))


def _attn(q, k, v, g, lam_vecs, subln_w, *, lambda_init, tq):
    b, s, _ = q.shape
    qblk = pl.BlockSpec((1, tq, DIFF_V_DIM), lambda bi, hi, qi: (bi, qi, hi))
    kvblk = pl.BlockSpec((1, s, DIFF_V_DIM), lambda bi, hi, qi: (bi, 0, hi))
    stat = pltpu.VMEM((2, 1, tq), F32)
    sbuf = pltpu.VMEM((2, tq, tq), F32)
    return pl.pallas_call(
        functools.partial(_attn_kernel, tq=tq, lambda_init=lambda_init),
        grid=(b, DIFF_HEADS, s // tq),
        in_specs=[_const_spec(lam_vecs.shape), qblk, kvblk, kvblk, qblk, _const_spec(subln_w.shape)],
        out_specs=qblk,
        out_shape=jax.ShapeDtypeStruct((b, s, DIFF_WIDTH), BF16),
        scratch_shapes=[pltpu.VMEM((2, tq, DIFF_V_DIM), BF16), sbuf, sbuf, stat, stat, stat, stat,
                        pltpu.VMEM((2, DIFF_V_DIM, tq), F32)],
        compiler_params=_params(("parallel", "parallel", "arbitrary")),
        name="diff_attn",
    )(lam_vecs, q, k, v, g, subln_w)


def _ple(h, p_ref, pw_ref, gw_ref):
    e = jnp.dot(p_ref[...].astype(BF16), pw_ref[...], preferred_element_type=F32)
    gate = jax.nn.sigmoid(jnp.dot(h.astype(BF16), gw_ref[...], preferred_element_type=F32))
    return h + e * gate


def _out0_kernel(y_ref, o_ref, x_ref, p_ref, wy_ref, wo_ref, pw_ref, gw_ref, h_ref):
    mix = (jnp.dot(y_ref[...], wy_ref[...], preferred_element_type=F32)
           + jnp.dot(o_ref[...], wo_ref[...], preferred_element_type=F32))
    h_ref[...] = _ple(x_ref[...] + mix, p_ref, pw_ref, gw_ref)


def _out0(y2, o2, x2, p2, wy, wo, pw, gw, *, tm):
    t, d = x2.shape
    row = lambda n: pl.BlockSpec((tm, n), lambda i: (i, 0))
    ws = [wy, wo, pw, gw]
    return pl.pallas_call(
        _out0_kernel,
        grid=(t // tm,),
        in_specs=[row(y2.shape[1]), row(o2.shape[1]), row(d), row(p2.shape[1])]
                 + [_const_spec(w.shape) for w in ws],
        out_specs=row(d),
        out_shape=jax.ShapeDtypeStruct((t, d), F32),
        compiler_params=_params(("parallel",)),
        name="out0",
    )(y2, o2, x2, p2, *ws)


def _proj1_kernel(h_ref, nw_ref, wu, wug, wg, u_o, g_o):
    hn = _rms_rows(h_ref[...], nw_ref[...]).astype(BF16)
    u = jnp.dot(hn, wu[...], preferred_element_type=F32)
    ug = jnp.dot(hn, wug[...], preferred_element_type=F32)
    u_o[...] = (u * jax.nn.sigmoid(ug)).astype(BF16)
    g_o[...] = jnp.dot(hn, wg[...], preferred_element_type=F32).astype(BF16)


def _proj1(h2, norm_w, wu, wug, wg, *, tm):
    t, d = h2.shape
    cw = wu.shape[1]
    row = lambda n: pl.BlockSpec((tm, n), lambda i: (i, 0))
    ws = [wu, wug, wg]
    return pl.pallas_call(
        _proj1_kernel,
        grid=(t // tm,),
        in_specs=[row(d), _const_spec((1, d))] + [_const_spec(w.shape) for w in ws],
        out_specs=[row(cw), row(cw)],
        out_shape=[jax.ShapeDtypeStruct((t, cw), BF16)] * 2,
        compiler_params=_params(("parallel",)),
        name="proj1",
    )(h2, norm_w, *ws)


def _conv_kernel(u_ref, halo_ref, g_ref, h_ref, p_ref, cw_ref, cb_ref, lnw_ref, lnb_ref,
                 wo_ref, pw_ref, gw_ref, fw_ref, out_ref, ubuf_sc, conv_sc, *, tm, halo, cblk):
    width = u_ref.shape[-1]
    first = pl.program_id(1) == 0
    halo_rows = halo_ref[0].astype(F32)
    ubuf_sc[0:halo, :] = jnp.where(first, jnp.zeros_like(halo_rows), halo_rows)
    ubuf_sc[halo:halo + tm, :] = u_ref[0].astype(F32)

    base = halo - (CONF_KERNEL - 1)
    rblk = min(tm, CONV_ROW_BLOCK)
    win_rows = rblk + halo

    def col_block(cb, carry):
        cs = pl.ds(pl.multiple_of(cb * cblk, cblk), cblk)
        for rb in range(tm // rblk):
            win = ubuf_sc[pl.ds(rb * rblk, win_rows), cs]
            acc = jnp.broadcast_to(cb_ref[:, cs], (rblk, cblk))
            for sh in range(SUBLANES):
                shifted = win if sh == 0 else pltpu.roll(win, win_rows - sh, axis=0)
                for m in range(halo // SUBLANES + 1):
                    j = m * SUBLANES + sh - base
                    if 0 <= j < CONF_KERNEL:
                        acc = acc + cw_ref[j:j + 1, cs] * shifted[m * SUBLANES:m * SUBLANES + rblk]
            conv_sc[pl.ds(rb * rblk, rblk), cs] = acc
        return carry

    lax.fori_loop(0, width // cblk, col_block, 0)

    c = conv_sc[...]
    mu = jnp.mean(c, axis=-1, keepdims=True)
    cc = c - mu
    var = jnp.mean(cc * cc, axis=-1, keepdims=True)
    ln = cc * lax.rsqrt(var + NORM_EPS) * lnw_ref[...] + lnb_ref[...]
    act = (_silu(ln) * _silu(g_ref[0].astype(F32))).astype(BF16)
    h = h_ref[0] + jnp.dot(act, wo_ref[...], preferred_element_type=F32)
    h = _ple(h, p_ref.at[0], pw_ref, gw_ref)
    out_ref[0] = _rms_rows(h, fw_ref[...])


def _conv_entry(u, g, h, p, conv_w, conv_b, ln_w, ln_b, wo, pw, gw, fw):
    row = lambda v: v.reshape(1, -1).astype(F32)
    halo = CONV_HALO
    conv_w = jnp.pad(conv_w.astype(F32), ((0, halo - CONF_KERNEL), (0, 0)))
    return _conv(u, g, h, p, conv_w, row(conv_b), row(ln_w), row(ln_b), wo, pw, gw, row(fw),
                 tm=_pick(u.shape[1], CONV_TILE), halo=halo, cblk=LANES)


def _conv(u, g, h, p, conv_w, conv_b, ln_w, ln_b, wo, pw, gw, fw, *, tm, halo, cblk):
    b, s, cw = u.shape
    d = h.shape[-1]
    per = tm // halo
    blk = lambda n: pl.BlockSpec((1, tm, n), lambda bi, i: (bi, i, 0))
    halo_spec = pl.BlockSpec((1, halo, cw), lambda bi, i: (bi, jnp.maximum(i * per - 1, 0), 0))
    consts = [conv_w, conv_b, ln_w, ln_b, wo, pw, gw, fw]
    return pl.pallas_call(
        functools.partial(_conv_kernel, tm=tm, halo=halo, cblk=cblk),
        grid=(b, s // tm),
        in_specs=[blk(cw), halo_spec, blk(cw), blk(d), blk(p.shape[-1])]
                 + [_const_spec(c.shape) for c in consts],
        out_specs=blk(d),
        out_shape=jax.ShapeDtypeStruct((b, s, d), F32),
        scratch_shapes=[pltpu.VMEM((halo + tm, cw), F32), pltpu.VMEM((tm, cw), F32)],
        compiler_params=_params(("parallel", "parallel")),
        name="conv_tail",
    )(u, u, g, h, p, *consts)


def _rope_tables(seq):
    half = DIFF_HEAD_DIM // 2
    pos = jnp.arange(seq, dtype=F32)
    inv = ROPE_THETA ** (-jnp.arange(0, DIFF_HEAD_DIM, 2, dtype=F32) / DIFF_HEAD_DIM)
    ang = pos[:, None] * inv[None, :]
    cos, sin = jnp.cos(ang), jnp.sin(ang)
    reps = LANES // DIFF_HEAD_DIM
    cos_t = jnp.tile(jnp.concatenate([cos, cos], axis=1), (1, reps))
    sin_t = jnp.tile(jnp.concatenate([-sin, sin], axis=1), (1, reps))
    return cos_t, sin_t


def _pick(n, pref):
    while n % pref:
        pref //= 2
    return pref


def kernel(x, p, norm_w, ple_w, ple_gate_w, even_w_in, ssd_conv_w, ssd_conv_b, ssd_dt_bias, ssd_a_log, ssd_d, ssd_norm_w, diff_lambda, diff_subln_w, even_w_out, conf_w_in, conf_conv_w, conf_conv_b, conf_ln_w, conf_ln_b, conf_w_out, final_norm_w):
    b, s, d = x.shape
    t = b * s
    x2 = x.reshape(t, d)
    row = lambda v: v.reshape(1, -1).astype(F32)

    w_in = even_w_in[0].astype(BF16)
    o_z, o_xbc = SSD_WIDTH, SSD_WIDTH + SSD_CONV_CH
    o_dt = o_xbc + SSD_HEADS
    o_q, o_k, o_v = o_dt + DIFF_WIDTH, o_dt + 2 * DIFF_WIDTH, o_dt + 3 * DIFF_WIDTH
    w_dt = jnp.pad(w_in[:, o_xbc:o_dt], ((0, 0), (0, LANES - SSD_HEADS)))
    ws0 = (w_in[:, :o_z], w_in[:, o_z:o_xbc], w_dt, w_in[:, o_dt:o_q], w_in[:, o_q:o_k],
           w_in[:, o_k:o_v], w_in[:, o_v:])
    cos_t, sin_t = _rope_tables(s)
    z, xbc, dt, q, k, v, g = _proj0(x2, row(norm_w[0]), cos_t, sin_t, ws0, seq=s,
                                    tm=_pick(s, PROJ0_TILE))

    y = _ssd_entry(xbc.reshape(b, s, -1), dt.reshape(b, s, -1), z.reshape(b, s, -1),
                   ssd_conv_w[0], ssd_conv_b[0], ssd_dt_bias[0], ssd_a_log[0], ssd_d[0],
                   ssd_norm_w[0])

    o = _attn_entry(q.reshape(b, s, -1), k.reshape(b, s, -1), v.reshape(b, s, -1),
                    g.reshape(b, s, -1), diff_lambda[0], diff_subln_w[0])

    w_out = even_w_out[0].astype(BF16)
    h1 = _out0(y.reshape(t, -1), o.reshape(t, -1), x2, p[0].reshape(t, -1),
               w_out[:SSD_WIDTH], w_out[SSD_WIDTH:], ple_w[0].astype(BF16),
               ple_gate_w[0].astype(BF16), tm=_pick(t, OUT0_TILE))

    w1 = conf_w_in[0].astype(BF16)
    cw = w1.shape[1] // 3
    u, g1 = _proj1(h1, row(norm_w[1]), w1[:, :cw], w1[:, cw:2 * cw], w1[:, 2 * cw:],
                   tm=_pick(t, PROJ1_TILE))
    return _conv_entry(u.reshape(b, s, cw), g1.reshape(b, s, cw), h1.reshape(b, s, d), p[1],
                       conf_conv_w[0], conf_conv_b[0], conf_ln_w[0], conf_ln_b[0],
                       conf_w_out[0].astype(BF16), ple_w[1].astype(BF16),
                       ple_gate_w[1].astype(BF16), final_norm_w)
```

```python
import functools
import math

import jax
import jax.numpy as jnp
from jax import lax
from jax.experimental import pallas as pl
from jax.experimental.pallas import tpu as pltpu

F32 = jnp.float32
BF16 = jnp.bfloat16

NORM_EPS = 1e-6
ROPE_THETA = 10000.0

SSD_HEADS = 16
SSD_HEAD_DIM = 64
SSD_WIDTH = SSD_HEADS * SSD_HEAD_DIM
SSD_GROUPS = 2
SSD_STATE = 128
SSD_CONV = 4
SSD_CHUNK = 128
SSD_BC = SSD_GROUPS * SSD_STATE
SSD_CONV_CH = SSD_WIDTH + 2 * SSD_BC
SSD_GROUP_WIDTH = SSD_WIDTH // SSD_GROUPS
SSD_CHUNKS_PER_STEP = 2
SSD_TAIL = 16

DIFF_HEADS = 8
DIFF_HEAD_DIM = 64
DIFF_V_DIM = 2 * DIFF_HEAD_DIM
DIFF_WIDTH = DIFF_HEADS * DIFF_V_DIM

CONF_KERNEL = 31

LANES = 128
SUBLANES = 8
VMEM_LIMIT = 48 * 1024 * 1024
MASK_VALUE = -1e30
ATTN_TILE = 1024
ATTN_SUM_ROWS = 16
PROJ0_TILE = 512
PROJ1_TILE = 256
OUT0_TILE = 512
CONV_TILE = 512
CONV_HALO = 32
CONV_ROW_BLOCK = 128

HIGHEST = lax.Precision.HIGHEST


def _silu(x):
    return x * jax.nn.sigmoid(x)


def _rms_rows(x, w):
    ms = jnp.mean(x * x, axis=-1, keepdims=True)
    return x * lax.rsqrt(ms + NORM_EPS) * w


def _const_spec(shape):
    return pl.BlockSpec(shape, lambda *_: (0,) * len(shape), pipeline_mode=pl.Buffered(1))


def _params(sem):
    return pltpu.CompilerParams(dimension_semantics=sem, vmem_limit_bytes=VMEM_LIMIT)


def _proj0_kernel(x_ref, nw_ref, cos_ref, sin_ref, wz, wxbc, wdt, wq, wk, wv, wg,
                  z_o, xbc_o, dt_o, q_o, k_o, v_o, g_o, *, q_scale):
    hn = _rms_rows(x_ref[...], nw_ref[...]).astype(BF16)

    def mm(w_ref):
        return jnp.dot(hn, w_ref[...], preferred_element_type=F32)

    z_o[...] = mm(wz).astype(BF16)
    xbc_o[...] = mm(wxbc).astype(BF16)
    dt_o[...] = mm(wdt)
    v_o[...] = mm(wv).astype(BF16)
    g_o[...] = mm(wg).astype(BF16)

    width = q_o.shape[-1]
    reps = width // LANES
    cos = jnp.tile(cos_ref[...], (1, reps))
    sin = jnp.tile(sin_ref[...], (1, reps))
    lane = lax.broadcasted_iota(jnp.int32, (1, width), 1)
    first_half = (lane % DIFF_HEAD_DIM) < (DIFF_HEAD_DIM // 2)

    def rope(t):
        partner = jnp.where(first_half,
                            pltpu.roll(t, width - DIFF_HEAD_DIM // 2, axis=1),
                            pltpu.roll(t, DIFF_HEAD_DIM // 2, axis=1))
        return t * cos + partner * sin

    q_o[...] = (rope(mm(wq)) * q_scale).astype(BF16)
    k_o[...] = rope(mm(wk)).astype(BF16)


def _proj0(x2, norm_w, cos_t, sin_t, ws, *, seq, tm):
    t, d = x2.shape
    wz, wxbc, wdt, wq, wk, wv, wg = ws
    nblk_seq = seq // tm
    row = lambda n: pl.BlockSpec((tm, n), lambda i: (i, 0))
    pos = pl.BlockSpec((tm, LANES), lambda i: (i % nblk_seq, 0))
    outs = [(SSD_WIDTH, BF16), (SSD_CONV_CH, BF16), (LANES, F32), (DIFF_WIDTH, BF16),
            (DIFF_WIDTH, BF16), (DIFF_WIDTH, BF16), (DIFF_WIDTH, BF16)]
    return pl.pallas_call(
        functools.partial(_proj0_kernel, q_scale=DIFF_HEAD_DIM ** -0.5 * math.log2(math.e)),
        grid=(t // tm,),
        in_specs=[row(d), _const_spec((1, d)), pos, pos] + [_const_spec(w.shape) for w in ws],
        out_specs=[row(n) for n, _ in outs],
        out_shape=[jax.ShapeDtypeStruct((t, n), dt) for n, dt in outs],
        compiler_params=_params(("parallel",)),
        name="proj0",
    )(x2, norm_w, cos_t, sin_t, *ws)


def _ssd_kernel(xbc_ref, halo_ref, dt_ref, z_ref, cw_ref, cb_ref, dtb_ref, aneg_ref,
                dskip_ref, nw_ref, expand_ref, shift_ref, y_ref, state_sc, *, chunks):
    L = SSD_CHUNK
    first = pl.program_id(1) == 0

    @pl.when(first)
    def _():
        state_sc[...] = jnp.zeros_like(state_sc)

    prev_rows = halo_ref[0]
    prev_rows = jnp.where(first, jnp.zeros_like(prev_rows), prev_rows)

    for c in range(chunks):
        lo_row = c * L
        if c == 0:
            win = jnp.concatenate([prev_rows, xbc_ref[0, 0:L, :]], axis=0)
        else:
            win = xbc_ref[0, lo_row - SSD_TAIL:lo_row + L, :]
        y = _ssd_chunk(win, dt_ref[0, lo_row:lo_row + L, :], z_ref[0, lo_row:lo_row + L, :],
                       cw_ref, cb_ref, dtb_ref, aneg_ref, dskip_ref, nw_ref, expand_ref, shift_ref,
                       state_sc)
        y_ref[0, lo_row:lo_row + L, :] = y


def _ssd_chunk(win, dt_raw, z_bf, cw_ref, cb_ref, dtb_ref, aneg_ref, dskip_ref, nw_ref,
               expand_ref, shift_ref, state_sc):
    L = SSD_CHUNK
    taps = jnp.dot(shift_ref[...], win, preferred_element_type=F32)
    acc = cb_ref[...] + cw_ref[0:1, :] * taps[0:L]
    for j in range(1, SSD_CONV):
        acc = acc + cw_ref[j:j + 1, :] * taps[j * L:(j + 1) * L]
    xbc = _silu(acc)
    xs = xbc[:, :SSD_WIDTH]
    bm = xbc[:, SSD_WIDTH:SSD_WIDTH + SSD_BC].astype(BF16)
    cm = xbc[:, SSD_WIDTH + SSD_BC:].astype(BF16)

    dt = jax.nn.softplus(dt_raw + dtb_ref[...])
    a = dt * aneg_ref[...]
    r_i = lax.broadcasted_iota(jnp.int32, (L, L), 0)
    c_i = lax.broadcasted_iota(jnp.int32, (L, L), 1)
    causal = r_i >= c_i
    tril = causal.astype(F32)
    acs = jnp.dot(tril, a, precision=HIGHEST, preferred_element_type=F32)
    acs_t = acs.T
    both = jnp.concatenate([dt, acs], axis=0)
    hi = both.astype(BF16)
    rest = both - hi.astype(F32)
    mid = rest.astype(BF16)
    lo = (rest - mid.astype(F32)).astype(BF16)
    both_x = jnp.dot(jnp.concatenate([hi, mid, lo], axis=1), expand_ref[...],
                     preferred_element_type=F32)
    dt_x = both_x[:L]
    acs_x = both_x[L:]
    acs_last = acs_x[L - 1:L, :]

    xd = xs * dt_x
    xd_bf = xd.astype(BF16)
    xdd_bf = (xd * jnp.exp(acs_last - acs_x)).astype(BF16)
    grow = jnp.exp(acs_x)
    chunk_decay = jnp.exp(acs_last)

    nt = (((1,), (1,)), ((), ()))
    tn = (((0,), (0,)), ((), ()))
    heads_per_group = SSD_HEADS // SSD_GROUPS
    quad_w = 2 * LANES
    heads_per_quad = quad_w // SSD_HEAD_DIM
    lane_q = lax.broadcasted_iota(jnp.int32, (1, quad_w), 1) // SSD_HEAD_DIM

    y_parts = []
    for g in range(SSD_GROUPS):
        bg = bm[:, g * SSD_STATE:(g + 1) * SSD_STATE]
        cg = cm[:, g * SSD_STATE:(g + 1) * SSD_STATE]
        gs = slice(g * SSD_GROUP_WIDTH, (g + 1) * SSD_GROUP_WIDTH)
        cb = lax.dot_general(cg, bg, nt, preferred_element_type=F32)
        prev = state_sc[g]
        y_g = jnp.dot(cg, prev.astype(BF16), preferred_element_type=F32) * grow[:, gs]
        new = lax.dot_general(bg, xdd_bf[:, gs], tn, preferred_element_type=F32)
        state_sc[g] = prev * chunk_decay[:, gs] + new
        quads = []
        for qd in range(SSD_GROUP_WIDTH // quad_w):
            lo = g * SSD_GROUP_WIDTH + qd * quad_w
            x_q = xd_bf[:, lo:lo + quad_w]
            y_q = jnp.zeros((L, quad_w), F32)
            for hq in range(heads_per_quad):
                h = g * heads_per_group + qd * heads_per_quad + hq
                diff = jnp.broadcast_to(acs[:, h:h + 1], (L, L)) - acs_t[h:h + 1, :]
                lmat = jnp.where(causal, jnp.exp(jnp.where(causal, diff, 0.0)), 0.0)
                m = (cb * lmat).astype(BF16)
                r = jnp.dot(m, x_q, preferred_element_type=F32)
                y_q = jnp.where(lane_q == hq, r, y_q)
            quads.append(y_q)
        y_parts.append(y_g + jnp.concatenate(quads, axis=1))

    z = z_bf.astype(F32)
    outs = []
    for g in range(SSD_GROUPS):
        gs = slice(g * SSD_GROUP_WIDTH, (g + 1) * SSD_GROUP_WIDTH)
        yz = (y_parts[g] + xs[:, gs] * dskip_ref[:, gs]) * _silu(z[:, gs])
        outs.append(_rms_rows(yz, nw_ref[:, gs]))
    return jnp.concatenate(outs, axis=1).astype(BF16)


def _ssd_entry(xbc, dt, z, conv_w, conv_b, dt_bias, a_log, d_skip, norm_w):
    row = lambda v: v.reshape(1, -1).astype(F32)
    pad_h = lambda vec: jnp.pad(vec.astype(F32), (0, LANES - SSD_HEADS)).reshape(1, LANES)
    head_of_channel = jnp.arange(SSD_WIDTH) // SSD_HEAD_DIM
    expand = (jnp.arange(LANES)[:, None] == head_of_channel[None, :]).astype(BF16)
    a_neg = -jnp.exp(a_log.astype(F32))
    return _ssd(xbc, dt, z, conv_w.astype(F32), row(conv_b), pad_h(dt_bias), pad_h(a_neg),
                row(d_skip[head_of_channel]), row(norm_w), jnp.tile(expand, (3, 1)),
                _shift_matrix(SSD_CONV, SSD_CHUNK, SSD_TAIL))


def _shift_matrix(taps, rows, tail):
    out_row = jnp.arange(taps * rows)
    src = out_row % rows + tail - (taps - 1 - out_row // rows)
    return (src[:, None] == jnp.arange(tail + rows)[None, :]).astype(BF16)


def _ssd(xbc, dt, z, conv_w, conv_b, dt_bias, a_neg, d_skip_x, norm_w, expand, shift):
    b, s, _ = xbc.shape
    chunks = _pick(s // SSD_CHUNK, SSD_CHUNKS_PER_STEP)
    rows = chunks * SSD_CHUNK
    blk = lambda n: pl.BlockSpec((1, rows, n), lambda bi, ci: (bi, ci, 0))
    consts = [conv_w, conv_b, dt_bias, a_neg, d_skip_x, norm_w, expand, shift]
    per = rows // SSD_TAIL
    halo = pl.BlockSpec((1, SSD_TAIL, SSD_CONV_CH),
                        lambda bi, ci: (bi, jnp.maximum(ci * per - 1, 0), 0))
    return pl.pallas_call(
        functools.partial(_ssd_kernel, chunks=chunks),
        grid=(b, s // rows),
        in_specs=[blk(SSD_CONV_CH), halo, blk(LANES), blk(SSD_WIDTH)]
                 + [_const_spec(c.shape) for c in consts],
        out_specs=blk(SSD_WIDTH),
        out_shape=jax.ShapeDtypeStruct((b, s, SSD_WIDTH), BF16),
        scratch_shapes=[pltpu.VMEM((SSD_GROUPS, SSD_STATE, SSD_GROUP_WIDTH), F32)],
        compiler_params=_params(("parallel", "arbitrary")),
        name="ssd",
    )(xbc, xbc, dt, z, *consts)


def _attn_kernel(lam_ref, q_ref, k_ref, v_ref, g_ref, sw_ref, o_ref,
                 qm_sc, sa_sc, sb_sc, mxa_sc, mxb_sc, m_sc, l_sc, acc_sc, *, tq, lambda_init):
    qi = pl.program_id(2)
    nt = (((1,), (1,)), ((), ()))

    q = q_ref[0]
    lane = lax.broadcasted_iota(jnp.int32, q.shape, 1)
    zero = jnp.zeros_like(q)
    qm_sc[0] = jnp.where(lane < DIFF_HEAD_DIM, q, zero)
    qm_sc[1] = jnp.where(lane >= DIFF_HEAD_DIM, q, zero)

    m_sc[...] = jnp.full(m_sc.shape, MASK_VALUE, F32)
    l_sc[...] = jnp.zeros(l_sc.shape, F32)
    acc_sc[...] = jnp.zeros(acc_sc.shape, F32)

    def scores(j, masked, s_ref, mx_ref):
        start = pl.multiple_of(j * tq, tq)
        k = k_ref[0, pl.ds(start, tq), :]
        for c in range(2):
            s = lax.dot_general(k, qm_sc[c], nt, preferred_element_type=F32)
            if masked:
                kpos = lax.broadcasted_iota(jnp.int32, s.shape, 0)
                qpos = lax.broadcasted_iota(jnp.int32, s.shape, 1)
                s = jnp.where(kpos <= qpos, s, MASK_VALUE)
            s_ref[c] = s
            mx_ref[c] = jnp.max(s, axis=0, keepdims=True)

    def consume(j, s_ref, mx_ref):
        start = pl.multiple_of(j * tq, tq)
        vt = jnp.concatenate([v_ref[0, pl.ds(start, tq), :].T,
                              jnp.ones((ATTN_SUM_ROWS, tq), BF16)], axis=0)
        for c in range(2):
            m_old = m_sc[c]
            m_new = jnp.maximum(m_old, mx_ref[c])
            p = jnp.exp2(s_ref[c] - m_new).astype(BF16)
            alpha = jnp.exp2(m_old - m_new)
            pv = jnp.dot(vt, p, preferred_element_type=F32)
            l_sc[c] = alpha * l_sc[c] + pv[DIFF_V_DIM:DIFF_V_DIM + 1]
            acc_sc[c] = alpha * acc_sc[c] + pv[:DIFF_V_DIM]
            m_sc[c] = m_new

    scores(qi, True, sa_sc, mxa_sc)

    def pair(i, carry):
        j = 2 * i
        scores(j, False, sb_sc, mxb_sc)
        consume(jnp.where(i == 0, qi, j - 1), sa_sc, mxa_sc)
        scores(j + 1, False, sa_sc, mxa_sc)
        consume(j, sb_sc, mxb_sc)
        return carry

    npairs = qi // 2
    lax.fori_loop(0, npairs, pair, 0)
    pending = jnp.where(npairs == 0, qi, 2 * npairs - 1)

    @pl.when(qi % 2 == 0)
    def _():
        consume(pending, sa_sc, mxa_sc)

    @pl.when(qi % 2 == 1)
    def _():
        scores(qi - 1, False, sb_sc, mxb_sc)
        consume(pending, sa_sc, mxa_sc)
        consume(qi - 1, sb_sc, mxb_sc)


    lv = lam_ref[...]
    lam = (jnp.exp(jnp.sum(lv[0:1] * lv[1:2], axis=1, keepdims=True))
           - jnp.exp(jnp.sum(lv[2:3] * lv[3:4], axis=1, keepdims=True)) + lambda_init)
    o_t = acc_sc[0] / l_sc[0] - lam * (acc_sc[1] / l_sc[1])
    o = _rms_rows(o_t.T, sw_ref[...]) * (1.0 - lambda_init)
    o_ref[0] = (o * _silu(g_ref[0].astype(F32))).astype(BF16)


def _attn_entry(q, k, v, g, lam_vecs, subln_w):
    lambda_init = 0.8 - 0.6 * math.exp(-0.3 * 0)
    return _attn(q, k, v, g, lam_vecs.astype(F32), subln_w.reshape(1, -1).astype(F32),
                 lambda_init=lambda_init, tq=_pick(q.shape[1], ATTN_TILE))


def _attn(q, k, v, g, lam_vecs, subln_w, *, lambda_init, tq):
    b, s, _ = q.shape
    qblk = pl.BlockSpec((1, tq, DIFF_V_DIM), lambda bi, hi, qi: (bi, qi, hi))
    kvblk = pl.BlockSpec((1, s, DIFF_V_DIM), lambda bi, hi, qi: (bi, 0, hi))
    stat = pltpu.VMEM((2, 1, tq), F32)
    sbuf = pltpu.VMEM((2, tq, tq), F32)
    return pl.pallas_call(
        functools.partial(_attn_kernel, tq=tq, lambda_init=lambda_init),
        grid=(b, DIFF_HEADS, s // tq),
        in_specs=[_const_spec(lam_vecs.shape), qblk, kvblk, kvblk, qblk, _const_spec(subln_w.shape)],
        out_specs=qblk,
        out_shape=jax.ShapeDtypeStruct((b, s, DIFF_WIDTH), BF16),
        scratch_shapes=[pltpu.VMEM((2, tq, DIFF_V_DIM), BF16), sbuf, sbuf, stat, stat, stat, stat,
                        pltpu.VMEM((2, DIFF_V_DIM, tq), F32)],
        compiler_params=_params(("parallel", "parallel", "arbitrary")),
        name="diff_attn",
    )(lam_vecs, q, k, v, g, subln_w)


def _ple(h, p_ref, pw_ref, gw_ref):
    e = jnp.dot(p_ref[...].astype(BF16), pw_ref[...], preferred_element_type=F32)
    gate = jax.nn.sigmoid(jnp.dot(h.astype(BF16), gw_ref[...], preferred_element_type=F32))
    return h + e * gate


def _out0_kernel(y_ref, o_ref, x_ref, p_ref, wy_ref, wo_ref, pw_ref, gw_ref, h_ref):
    mix = (jnp.dot(y_ref[...], wy_ref[...], preferred_element_type=F32)
           + jnp.dot(o_ref[...], wo_ref[...], preferred_element_type=F32))
    h_ref[...] = _ple(x_ref[...] + mix, p_ref, pw_ref, gw_ref)


def _out0(y2, o2, x2, p2, wy, wo, pw, gw, *, tm):
    t, d = x2.shape
    row = lambda n: pl.BlockSpec((tm, n), lambda i: (i, 0))
    ws = [wy, wo, pw, gw]
    return pl.pallas_call(
        _out0_kernel,
        grid=(t // tm,),
        in_specs=[row(y2.shape[1]), row(o2.shape[1]), row(d), row(p2.shape[1])]
                 + [_const_spec(w.shape) for w in ws],
        out_specs=row(d),
        out_shape=jax.ShapeDtypeStruct((t, d), F32),
        compiler_params=_params(("parallel",)),
        name="out0",
    )(y2, o2, x2, p2, *ws)


def _proj1_kernel(h_ref, nw_ref, wu, wug, wg, u_o, g_o):
    hn = _rms_rows(h_ref[...], nw_ref[...]).astype(BF16)
    u = jnp.dot(hn, wu[...], preferred_element_type=F32)
    ug = jnp.dot(hn, wug[...], preferred_element_type=F32)
    u_o[...] = (u * jax.nn.sigmoid(ug)).astype(BF16)
    g_o[...] = jnp.dot(hn, wg[...], preferred_element_type=F32).astype(BF16)


def _proj1(h2, norm_w, wu, wug, wg, *, tm):
    t, d = h2.shape
    cw = wu.shape[1]
    row = lambda n: pl.BlockSpec((tm, n), lambda i: (i, 0))
    ws = [wu, wug, wg]
    return pl.pallas_call(
        _proj1_kernel,
        grid=(t // tm,),
        in_specs=[row(d), _const_spec((1, d))] + [_const_spec(w.shape) for w in ws],
        out_specs=[row(cw), row(cw)],
        out_shape=[jax.ShapeDtypeStruct((t, cw), BF16)] * 2,
        compiler_params=_params(("parallel",)),
        name="proj1",
    )(h2, norm_w, *ws)


def _conv_kernel(u_ref, halo_ref, g_ref, h_ref, p_ref, cw_ref, cb_ref, lnw_ref, lnb_ref,
                 wo_ref, pw_ref, gw_ref, fw_ref, out_ref, ubuf_sc, conv_sc, *, tm, halo, cblk):
    width = u_ref.shape[-1]
    first = pl.program_id(1) == 0
    halo_rows = halo_ref[0].astype(F32)
    ubuf_sc[0:halo, :] = jnp.where(first, jnp.zeros_like(halo_rows), halo_rows)
    ubuf_sc[halo:halo + tm, :] = u_ref[0].astype(F32)

    base = halo - (CONF_KERNEL - 1)
    rblk = min(tm, CONV_ROW_BLOCK)
    win_rows = rblk + halo

    def col_block(cb, carry):
        cs = pl.ds(pl.multiple_of(cb * cblk, cblk), cblk)
        for rb in range(tm // rblk):
            win = ubuf_sc[pl.ds(rb * rblk, win_rows), cs]
            acc = jnp.broadcast_to(cb_ref[:, cs], (rblk, cblk))
            for sh in range(SUBLANES):
                shifted = win if sh == 0 else pltpu.roll(win, win_rows - sh, axis=0)
                for m in range(halo // SUBLANES + 1):
                    j = m * SUBLANES + sh - base
                    if 0 <= j < CONF_KERNEL:
                        acc = acc + cw_ref[j:j + 1, cs] * shifted[m * SUBLANES:m * SUBLANES + rblk]
            conv_sc[pl.ds(rb * rblk, rblk), cs] = acc
        return carry

    lax.fori_loop(0, width // cblk, col_block, 0)

    c = conv_sc[...]
    mu = jnp.mean(c, axis=-1, keepdims=True)
    cc = c - mu
    var = jnp.mean(cc * cc, axis=-1, keepdims=True)
    ln = cc * lax.rsqrt(var + NORM_EPS) * lnw_ref[...] + lnb_ref[...]
    act = (_silu(ln) * _silu(g_ref[0].astype(F32))).astype(BF16)
    h = h_ref[0] + jnp.dot(act, wo_ref[...], preferred_element_type=F32)
    h = _ple(h, p_ref.at[0], pw_ref, gw_ref)
    out_ref[0] = _rms_rows(h, fw_ref[...])


def _conv_entry(u, g, h, p, conv_w, conv_b, ln_w, ln_b, wo, pw, gw, fw):
    row = lambda v: v.reshape(1, -1).astype(F32)
    halo = CONV_HALO
    conv_w = jnp.pad(conv_w.astype(F32), ((0, halo - CONF_KERNEL), (0, 0)))
    return _conv(u, g, h, p, conv_w, row(conv_b), row(ln_w), row(ln_b), wo, pw, gw, row(fw),
                 tm=_pick(u.shape[1], CONV_TILE), halo=halo, cblk=LANES)


def _conv(u, g, h, p, conv_w, conv_b, ln_w, ln_b, wo, pw, gw, fw, *, tm, halo, cblk):
    b, s, cw = u.shape
    d = h.shape[-1]
    per = tm // halo
    blk = lambda n: pl.BlockSpec((1, tm, n), lambda bi, i: (bi, i, 0))
    halo_spec = pl.BlockSpec((1, halo, cw), lambda bi, i: (bi, jnp.maximum(i * per - 1, 0), 0))
    consts = [conv_w, conv_b, ln_w, ln_b, wo, pw, gw, fw]
    return pl.pallas_call(
        functools.partial(_conv_kernel, tm=tm, halo=halo, cblk=cblk),
        grid=(b, s // tm),
        in_specs=[blk(cw), halo_spec, blk(cw), blk(d), blk(p.shape[-1])]
                 + [_const_spec(c.shape) for c in consts],
        out_specs=blk(d),
        out_shape=jax.ShapeDtypeStruct((b, s, d), F32),
        scratch_shapes=[pltpu.VMEM((halo + tm, cw), F32), pltpu.VMEM((tm, cw), F32)],
        compiler_params=_params(("parallel", "parallel")),
        name="conv_tail",
    )(u, u, g, h, p, *consts)


def _rope_tables(seq):
    half = DIFF_HEAD_DIM // 2
    pos = jnp.arange(seq, dtype=F32)
    inv = ROPE_THETA ** (-jnp.arange(0, DIFF_HEAD_DIM, 2, dtype=F32) / DIFF_HEAD_DIM)
    ang = pos[:, None] * inv[None, :]
    cos, sin = jnp.cos(ang), jnp.sin(ang)
    reps = LANES // DIFF_HEAD_DIM
    cos_t = jnp.tile(jnp.concatenate([cos, cos], axis=1), (1, reps))
    sin_t = jnp.tile(jnp.concatenate([-sin, sin], axis=1), (1, reps))
    return cos_t, sin_t


def _pick(n, pref):
    while n % pref:
        pref //= 2
    return pref


def kernel(x, p, norm_w, ple_w, ple_gate_w, even_w_in, ssd_conv_w, ssd_conv_b, ssd_dt_bias, ssd_a_log, ssd_d, ssd_norm_w, diff_lambda, diff_subln_w, even_w_out, conf_w_in, conf_conv_w, conf_conv_b, conf_ln_w, conf_ln_b, conf_w_out, final_norm_w):
    b, s, d = x.shape
    t = b * s
    x2 = x.reshape(t, d)
    row = lambda v: v.reshape(1, -1).astype(F32)

    w_in = even_w_in[0].astype(BF16)
    o_z, o_xbc = SSD_WIDTH, SSD_WIDTH + SSD_CONV_CH
    o_dt = o_xbc + SSD_HEADS
    o_q, o_k, o_v = o_dt + DIFF_WIDTH, o_dt + 2 * DIFF_WIDTH, o_dt + 3 * DIFF_WIDTH
    w_dt = jnp.pad(w_in[:, o_xbc:o_dt], ((0, 0), (0, LANES - SSD_HEADS)))
    ws0 = (w_in[:, :o_z], w_in[:, o_z:o_xbc], w_dt, w_in[:, o_dt:o_q], w_in[:, o_q:o_k],
           w_in[:, o_k:o_v], w_in[:, o_v:])
    cos_t, sin_t = _rope_tables(s)
    z, xbc, dt, q, k, v, g = _proj0(x2, row(norm_w[0]), cos_t, sin_t, ws0, seq=s,
                                    tm=_pick(s, PROJ0_TILE))

    y = _ssd_entry(xbc.reshape(b, s, -1), dt.reshape(b, s, -1), z.reshape(b, s, -1),
                   ssd_conv_w[0], ssd_conv_b[0], ssd_dt_bias[0], ssd_a_log[0], ssd_d[0],
                   ssd_norm_w[0])

    o = _attn_entry(q.reshape(b, s, -1), k.reshape(b, s, -1), v.reshape(b, s, -1),
                    g.reshape(b, s, -1), diff_lambda[0], diff_subln_w[0])

    w_out = even_w_out[0].astype(BF16)
    h1 = _out0(y.reshape(t, -1), o.reshape(t, -1), x2, p[0].reshape(t, -1),
               w_out[:SSD_WIDTH], w_out[SSD_WIDTH:], ple_w[0].astype(BF16),
               ple_gate_w[0].astype(BF16), tm=_pick(t, OUT0_TILE))

    w1 = conf_w_in[0].astype(BF16)
    cw = w1.shape[1] // 3
    u, g1 = _proj1(h1, row(norm_w[1]), w1[:, :cw], w1[:, cw:2 * cw], w1[:, 2 * cw:],
                   tm=_pick(t, PROJ1_TILE))
    return _conv_entry(u.reshape(b, s, cw), g1.reshape(b, s, cw), h1.reshape(b, s, d), p[1],
                       conf_conv_w[0], conf_conv_b[0], conf_ln_w[0], conf_ln_b[0],
                       conf_w_out[0].astype(BF16), ple_w[1].astype(BF16),
                       ple_gate_w[1].astype(BF16), final_norm_w)
```

```python
import functools
import math

import jax
import jax.numpy as jnp
from jax import lax
from jax.experimental import pallas as pl
from jax.experimental.pallas import tpu as pltpu

F32 = jnp.float32
BF16 = jnp.bfloat16

NORM_EPS = 1e-6
ROPE_THETA = 10000.0

SSD_HEADS = 16
SSD_HEAD_DIM = 64
SSD_WIDTH = SSD_HEADS * SSD_HEAD_DIM
SSD_GROUPS = 2
SSD_STATE = 128
SSD_CONV = 4
SSD_CHUNK = 128
SSD_BC = SSD_GROUPS * SSD_STATE
SSD_CONV_CH = SSD_WIDTH + 2 * SSD_BC
SSD_GROUP_WIDTH = SSD_WIDTH // SSD_GROUPS
SSD_CHUNKS_PER_STEP = 2
SSD_TAIL = 16

DIFF_HEADS = 8
DIFF_HEAD_DIM = 64
DIFF_V_DIM = 2 * DIFF_HEAD_DIM
DIFF_WIDTH = DIFF_HEADS * DIFF_V_DIM

CONF_KERNEL = 31

LANES = 128
SUBLANES = 8
VMEM_LIMIT = 48 * 1024 * 1024
MASK_VALUE = -1e30
ATTN_TILE = 1024
ATTN_SUM_ROWS = 16
PROJ0_TILE = 512
PROJ1_TILE = 256
OUT0_TILE = 512
CONV_TILE = 512
CONV_HALO = 32
CONV_STRIDE = 4

HIGHEST = lax.Precision.HIGHEST


def _silu(x):
    return x * jax.nn.sigmoid(x)


def _rms_rows(x, w):
    ms = jnp.mean(x * x, axis=-1, keepdims=True)
    return x * lax.rsqrt(ms + NORM_EPS) * w


def _const_spec(shape):
    return pl.BlockSpec(shape, lambda *_: (0,) * len(shape), pipeline_mode=pl.Buffered(1))


def _params(sem):
    return pltpu.CompilerParams(dimension_semantics=sem, vmem_limit_bytes=VMEM_LIMIT)


def _proj0_kernel(x_ref, nw_ref, cos_ref, sin_ref, wz, wxbc, wdt, wq, wk, wv, wg,
                  z_o, xbc_o, dt_o, q_o, k_o, v_o, g_o, *, q_scale):
    hn = _rms_rows(x_ref[...], nw_ref[...]).astype(BF16)

    def mm(w_ref):
        return jnp.dot(hn, w_ref[...], preferred_element_type=F32)

    z_o[...] = mm(wz).astype(BF16)
    xbc_o[...] = mm(wxbc).astype(BF16)
    dt_o[...] = mm(wdt)
    v_o[...] = mm(wv).astype(BF16)
    g_o[...] = mm(wg).astype(BF16)

    width = q_o.shape[-1]
    reps = width // LANES
    cos = jnp.tile(cos_ref[...], (1, reps))
    sin = jnp.tile(sin_ref[...], (1, reps))
    lane = lax.broadcasted_iota(jnp.int32, (1, width), 1)
    first_half = (lane % DIFF_HEAD_DIM) < (DIFF_HEAD_DIM // 2)

    def rope(t):
        partner = jnp.where(first_half,
                            pltpu.roll(t, width - DIFF_HEAD_DIM // 2, axis=1),
                            pltpu.roll(t, DIFF_HEAD_DIM // 2, axis=1))
        return t * cos + partner * sin

    q_o[...] = (rope(mm(wq)) * q_scale).astype(BF16)
    k_o[...] = rope(mm(wk)).astype(BF16)


def _proj0(x2, norm_w, cos_t, sin_t, ws, *, seq, tm):
    t, d = x2.shape
    wz, wxbc, wdt, wq, wk, wv, wg = ws
    nblk_seq = seq // tm
    row = lambda n: pl.BlockSpec((tm, n), lambda i: (i, 0))
    pos = pl.BlockSpec((tm, LANES), lambda i: (i % nblk_seq, 0))
    outs = [(SSD_WIDTH, BF16), (SSD_CONV_CH, BF16), (LANES, F32), (DIFF_WIDTH, BF16),
            (DIFF_WIDTH, BF16), (DIFF_WIDTH, BF16), (DIFF_WIDTH, BF16)]
    return pl.pallas_call(
        functools.partial(_proj0_kernel, q_scale=DIFF_HEAD_DIM ** -0.5 * math.log2(math.e)),
        grid=(t // tm,),
        in_specs=[row(d), _const_spec((1, d)), pos, pos] + [_const_spec(w.shape) for w in ws],
        out_specs=[row(n) for n, _ in outs],
        out_shape=[jax.ShapeDtypeStruct((t, n), dt) for n, dt in outs],
        compiler_params=_params(("parallel",)),
        name="proj0",
    )(x2, norm_w, cos_t, sin_t, *ws)


def _ssd_kernel(xbc_ref, halo_ref, dt_ref, z_ref, cw_ref, cb_ref, dtb_ref, aneg_ref,
                dskip_ref, nw_ref, expand_ref, shift_ref, y_ref, state_sc, *, chunks):
    L = SSD_CHUNK
    first = pl.program_id(1) == 0

    @pl.when(first)
    def _():
        state_sc[...] = jnp.zeros_like(state_sc)

    prev_rows = halo_ref[0]
    prev_rows = jnp.where(first, jnp.zeros_like(prev_rows), prev_rows)

    for c in range(chunks):
        lo_row = c * L
        if c == 0:
            win = jnp.concatenate([prev_rows, xbc_ref[0, 0:L, :]], axis=0)
        else:
            win = xbc_ref[0, lo_row - SSD_TAIL:lo_row + L, :]
        y = _ssd_chunk(win, dt_ref[0, lo_row:lo_row + L, :], z_ref[0, lo_row:lo_row + L, :],
                       cw_ref, cb_ref, dtb_ref, aneg_ref, dskip_ref, nw_ref, expand_ref, shift_ref,
                       state_sc)
        y_ref[0, lo_row:lo_row + L, :] = y


def _ssd_chunk(win, dt_raw, z_bf, cw_ref, cb_ref, dtb_ref, aneg_ref, dskip_ref, nw_ref,
               expand_ref, shift_ref, state_sc):
    L = SSD_CHUNK
    taps = jnp.dot(shift_ref[...], win, preferred_element_type=F32)
    acc = cb_ref[...] + cw_ref[0:1, :] * taps[0:L]
    for j in range(1, SSD_CONV):
        acc = acc + cw_ref[j:j + 1, :] * taps[j * L:(j + 1) * L]
    xbc = _silu(acc)
    xs = xbc[:, :SSD_WIDTH]
    bm = xbc[:, SSD_WIDTH:SSD_WIDTH + SSD_BC].astype(BF16)
    cm = xbc[:, SSD_WIDTH + SSD_BC:].astype(BF16)

    dt = jax.nn.softplus(dt_raw + dtb_ref[...])
    a = dt * aneg_ref[...]
    r_i = lax.broadcasted_iota(jnp.int32, (L, L), 0)
    c_i = lax.broadcasted_iota(jnp.int32, (L, L), 1)
    causal = r_i >= c_i
    tril = causal.astype(F32)
    acs = jnp.dot(tril, a, precision=HIGHEST, preferred_element_type=F32)
    acs_t = acs.T
    both = jnp.concatenate([dt, acs], axis=0)
    hi = both.astype(BF16)
    rest = both - hi.astype(F32)
    mid = rest.astype(BF16)
    lo = (rest - mid.astype(F32)).astype(BF16)
    both_x = jnp.dot(jnp.concatenate([hi, mid, lo], axis=1), expand_ref[...],
                     preferred_element_type=F32)
    dt_x = both_x[:L]
    acs_x = both_x[L:]
    acs_last = acs_x[L - 1:L, :]

    xd = xs * dt_x
    xd_bf = xd.astype(BF16)
    xdd_bf = (xd * jnp.exp(acs_last - acs_x)).astype(BF16)
    grow = jnp.exp(acs_x)
    chunk_decay = jnp.exp(acs_last)

    nt = (((1,), (1,)), ((), ()))
    tn = (((0,), (0,)), ((), ()))
    heads_per_group = SSD_HEADS // SSD_GROUPS
    quad_w = 2 * LANES
    heads_per_quad = quad_w // SSD_HEAD_DIM
    lane_q = lax.broadcasted_iota(jnp.int32, (1, quad_w), 1) // SSD_HEAD_DIM

    y_parts = []
    for g in range(SSD_GROUPS):
        bg = bm[:, g * SSD_STATE:(g + 1) * SSD_STATE]
        cg = cm[:, g * SSD_STATE:(g + 1) * SSD_STATE]
        gs = slice(g * SSD_GROUP_WIDTH, (g + 1) * SSD_GROUP_WIDTH)
        cb = lax.dot_general(cg, bg, nt, preferred_element_type=F32)
        prev = state_sc[g]
        y_g = jnp.dot(cg, prev.astype(BF16), preferred_element_type=F32) * grow[:, gs]
        new = lax.dot_general(bg, xdd_bf[:, gs], tn, preferred_element_type=F32)
        state_sc[g] = prev * chunk_decay[:, gs] + new
        quads = []
        for qd in range(SSD_GROUP_WIDTH // quad_w):
            lo = g * SSD_GROUP_WIDTH + qd * quad_w
            x_q = xd_bf[:, lo:lo + quad_w]
            y_q = jnp.zeros((L, quad_w), F32)
            for hq in range(heads_per_quad):
                h = g * heads_per_group + qd * heads_per_quad + hq
                diff = jnp.broadcast_to(acs[:, h:h + 1], (L, L)) - acs_t[h:h + 1, :]
                lmat = jnp.where(causal, jnp.exp(jnp.where(causal, diff, 0.0)), 0.0)
                m = (cb * lmat).astype(BF16)
                r = jnp.dot(m, x_q, preferred_element_type=F32)
                y_q = jnp.where(lane_q == hq, r, y_q)
            quads.append(y_q)
        y_parts.append(y_g + jnp.concatenate(quads, axis=1))

    z = z_bf.astype(F32)
    outs = []
    for g in range(SSD_GROUPS):
        gs = slice(g * SSD_GROUP_WIDTH, (g + 1) * SSD_GROUP_WIDTH)
        yz = (y_parts[g] + xs[:, gs] * dskip_ref[:, gs]) * _silu(z[:, gs])
        outs.append(_rms_rows(yz, nw_ref[:, gs]))
    return jnp.concatenate(outs, axis=1).astype(BF16)


def _ssd_entry(xbc, dt, z, conv_w, conv_b, dt_bias, a_log, d_skip, norm_w):
    row = lambda v: v.reshape(1, -1).astype(F32)
    pad_h = lambda vec: jnp.pad(vec.astype(F32), (0, LANES - SSD_HEADS)).reshape(1, LANES)
    head_of_channel = jnp.arange(SSD_WIDTH) // SSD_HEAD_DIM
    expand = (jnp.arange(LANES)[:, None] == head_of_channel[None, :]).astype(BF16)
    a_neg = -jnp.exp(a_log.astype(F32))
    return _ssd(xbc, dt, z, conv_w.astype(F32), row(conv_b), pad_h(dt_bias), pad_h(a_neg),
                row(d_skip[head_of_channel]), row(norm_w), jnp.tile(expand, (3, 1)),
                _shift_matrix(SSD_CONV, SSD_CHUNK, SSD_TAIL))


def _shift_matrix(taps, rows, tail):
    out_row = jnp.arange(taps * rows)
    src = out_row % rows + tail - (taps - 1 - out_row // rows)
    return (src[:, None] == jnp.arange(tail + rows)[None, :]).astype(BF16)


def _ssd(xbc, dt, z, conv_w, conv_b, dt_bias, a_neg, d_skip_x, norm_w, expand, shift):
    b, s, _ = xbc.shape
    chunks = _pick(s // SSD_CHUNK, SSD_CHUNKS_PER_STEP)
    rows = chunks * SSD_CHUNK
    blk = lambda n: pl.BlockSpec((1, rows, n), lambda bi, ci: (bi, ci, 0))
    consts = [conv_w, conv_b, dt_bias, a_neg, d_skip_x, norm_w, expand, shift]
    per = rows // SSD_TAIL
    halo = pl.BlockSpec((1, SSD_TAIL, SSD_CONV_CH),
                        lambda bi, ci: (bi, jnp.maximum(ci * per - 1, 0), 0))
    return pl.pallas_call(
        functools.partial(_ssd_kernel, chunks=chunks),
        grid=(b, s // rows),
        in_specs=[blk(SSD_CONV_CH), halo, blk(LANES), blk(SSD_WIDTH)]
                 + [_const_spec(c.shape) for c in consts],
        out_specs=blk(SSD_WIDTH),
        out_shape=jax.ShapeDtypeStruct((b, s, SSD_WIDTH), BF16),
        scratch_shapes=[pltpu.VMEM((SSD_GROUPS, SSD_STATE, SSD_GROUP_WIDTH), F32)],
        compiler_params=_params(("parallel", "arbitrary")),
        name="ssd",
    )(xbc, xbc, dt, z, *consts)


def _attn_kernel(lam_ref, q_ref, k_ref, v_ref, g_ref, sw_ref, o_ref,
                 qm_sc, sa_sc, sb_sc, mxa_sc, mxb_sc, m_sc, l_sc, acc_sc, *, tq, lambda_init):
    qi = pl.program_id(2)
    nt = (((1,), (1,)), ((), ()))

    q = q_ref[0]
    lane = lax.broadcasted_iota(jnp.int32, q.shape, 1)
    zero = jnp.zeros_like(q)
    qm_sc[0] = jnp.where(lane < DIFF_HEAD_DIM, q, zero)
    qm_sc[1] = jnp.where(lane >= DIFF_HEAD_DIM, q, zero)

    m_sc[...] = jnp.full(m_sc.shape, MASK_VALUE, F32)
    l_sc[...] = jnp.zeros(l_sc.shape, F32)
    acc_sc[...] = jnp.zeros(acc_sc.shape, F32)

    def scores(j, masked, s_ref, mx_ref):
        start = pl.multiple_of(j * tq, tq)
        k = k_ref[0, pl.ds(start, tq), :]
        for c in range(2):
            s = lax.dot_general(k, qm_sc[c], nt, preferred_element_type=F32)
            if masked:
                kpos = lax.broadcasted_iota(jnp.int32, s.shape, 0)
                qpos = lax.broadcasted_iota(jnp.int32, s.shape, 1)
                s = jnp.where(kpos <= qpos, s, MASK_VALUE)
            s_ref[c] = s
            mx_ref[c] = jnp.max(s, axis=0, keepdims=True)

    def consume(j, s_ref, mx_ref):
        start = pl.multiple_of(j * tq, tq)
        vt = jnp.concatenate([v_ref[0, pl.ds(start, tq), :].T,
                              jnp.ones((ATTN_SUM_ROWS, tq), BF16)], axis=0)
        for c in range(2):
            m_old = m_sc[c]
            m_new = jnp.maximum(m_old, mx_ref[c])
            p = jnp.exp2(s_ref[c] - m_new).astype(BF16)
            alpha = jnp.exp2(m_old - m_new)
            pv = jnp.dot(vt, p, preferred_element_type=F32)
            l_sc[c] = alpha * l_sc[c] + pv[DIFF_V_DIM:DIFF_V_DIM + 1]
            acc_sc[c] = alpha * acc_sc[c] + pv[:DIFF_V_DIM]
            m_sc[c] = m_new

    scores(qi, True, sa_sc, mxa_sc)

    def pair(i, carry):
        j = 2 * i
        scores(j, False, sb_sc, mxb_sc)
        consume(jnp.where(i == 0, qi, j - 1), sa_sc, mxa_sc)
        scores(j + 1, False, sa_sc, mxa_sc)
        consume(j, sb_sc, mxb_sc)
        return carry

    npairs = qi // 2
    lax.fori_loop(0, npairs, pair, 0)
    pending = jnp.where(npairs == 0, qi, 2 * npairs - 1)

    @pl.when(qi % 2 == 0)
    def _():
        consume(pending, sa_sc, mxa_sc)

    @pl.when(qi % 2 == 1)
    def _():
        scores(qi - 1, False, sb_sc, mxb_sc)
        consume(pending, sa_sc, mxa_sc)
        consume(qi - 1, sb_sc, mxb_sc)


    lv = lam_ref[...]
    lam = (jnp.exp(jnp.sum(lv[0:1] * lv[1:2], axis=1, keepdims=True))
           - jnp.exp(jnp.sum(lv[2:3] * lv[3:4], axis=1, keepdims=True)) + lambda_init)
    o_t = acc_sc[0] / l_sc[0] - lam * (acc_sc[1] / l_sc[1])
    o = _rms_rows(o_t.T, sw_ref[...]) * (1.0 - lambda_init)
    o_ref[0] = (o * _silu(g_ref[0].astype(F32))).astype(BF16)


def _attn_entry(q, k, v, g, lam_vecs, subln_w):
    lambda_init = 0.8 - 0.6 * math.exp(-0.3 * 0)
    return _attn(q, k, v, g, lam_vecs.astype(F32), subln_w.reshape(1, -1).astype(F32),
                 lambda_init=lambda_init, tq=_pick(q.shape[1], ATTN_TILE))


def _attn(q, k, v, g, lam_vecs, subln_w, *, lambda_init, tq):
    b, s, _ = q.shape
    qblk = pl.BlockSpec((1, tq, DIFF_V_DIM), lambda bi, hi, qi: (bi, qi, hi))
    kvblk = pl.BlockSpec((1, s, DIFF_V_DIM), lambda bi, hi, qi: (bi, 0, hi))
    stat = pltpu.VMEM((2, 1, tq), F32)
    sbuf = pltpu.VMEM((2, tq, tq), F32)
    return pl.pallas_call(
        functools.partial(_attn_kernel, tq=tq, lambda_init=lambda_init),
        grid=(b, DIFF_HEADS, s // tq),
        in_specs=[_const_spec(lam_vecs.shape), qblk, kvblk, kvblk, qblk, _const_spec(subln_w.shape)],
        out_specs=qblk,
        out_shape=jax.ShapeDtypeStruct((b, s, DIFF_WIDTH), BF16),
        scratch_shapes=[pltpu.VMEM((2, tq, DIFF_V_DIM), BF16), sbuf, sbuf, stat, stat, stat, stat,
                        pltpu.VMEM((2, DIFF_V_DIM, tq), F32)],
        compiler_params=_params(("parallel", "parallel", "arbitrary")),
        name="diff_attn",
    )(lam_vecs, q, k, v, g, subln_w)


def _ple(h, p_ref, pw_ref, gw_ref):
    e = jnp.dot(p_ref[...].astype(BF16), pw_ref[...], preferred_element_type=F32)
    gate = jax.nn.sigmoid(jnp.dot(h.astype(BF16), gw_ref[...], preferred_element_type=F32))
    return h + e * gate


def _out0_kernel(y_ref, o_ref, x_ref, p_ref, wy_ref, wo_ref, pw_ref, gw_ref, h_ref):
    mix = (jnp.dot(y_ref[...], wy_ref[...], preferred_element_type=F32)
           + jnp.dot(o_ref[...], wo_ref[...], preferred_element_type=F32))
    h_ref[...] = _ple(x_ref[...] + mix, p_ref, pw_ref, gw_ref)


def _out0(y2, o2, x2, p2, wy, wo, pw, gw, *, tm):
    t, d = x2.shape
    row = lambda n: pl.BlockSpec((tm, n), lambda i: (i, 0))
    ws = [wy, wo, pw, gw]
    return pl.pallas_call(
        _out0_kernel,
        grid=(t // tm,),
        in_specs=[row(y2.shape[1]), row(o2.shape[1]), row(d), row(p2.shape[1])]
                 + [_const_spec(w.shape) for w in ws],
        out_specs=row(d),
        out_shape=jax.ShapeDtypeStruct((t, d), F32),
        compiler_params=_params(("parallel",)),
        name="out0",
    )(y2, o2, x2, p2, *ws)


def _proj1_kernel(h_ref, nw_ref, wu, wug, wg, u_o, g_o):
    hn = _rms_rows(h_ref[...], nw_ref[...]).astype(BF16)
    u = jnp.dot(hn, wu[...], preferred_element_type=F32)
    ug = jnp.dot(hn, wug[...], preferred_element_type=F32)
    u_o[...] = (u * jax.nn.sigmoid(ug)).astype(BF16)
    g_o[...] = jnp.dot(hn, wg[...], preferred_element_type=F32).astype(BF16)


def _proj1(h2, norm_w, wu, wug, wg, *, tm):
    t, d = h2.shape
    cw = wu.shape[1]
    row = lambda n: pl.BlockSpec((tm, n), lambda i: (i, 0))
    ws = [wu, wug, wg]
    return pl.pallas_call(
        _proj1_kernel,
        grid=(t // tm,),
        in_specs=[row(d), _const_spec((1, d))] + [_const_spec(w.shape) for w in ws],
        out_specs=[row(cw), row(cw)],
        out_shape=[jax.ShapeDtypeStruct((t, cw), BF16)] * 2,
        compiler_params=_params(("parallel",)),
        name="proj1",
    )(h2, norm_w, *ws)


def _conv_kernel(u_ref, halo_ref, g_ref, h_ref, p_ref, cw_ref, cb_ref, lnw_ref, lnb_ref,
                 wo_ref, pw_ref, gw_ref, fw_ref, out_ref, ubuf_sc, conv_sc, *, tm, halo):
    width = u_ref.shape[-1]
    nslab = width // LANES
    first = pl.program_id(1) == 0
    halo_rows = halo_ref[0].astype(F32)
    halo_rows = jnp.where(first, jnp.zeros_like(halo_rows), halo_rows)
    u_rows = u_ref[0].astype(F32)
    for sl in range(nslab):
        ubuf_sc[sl, 0:halo, :] = halo_rows[:, sl * LANES:(sl + 1) * LANES]
        ubuf_sc[sl, halo:halo + tm, :] = u_rows[:, sl * LANES:(sl + 1) * LANES]

    base = halo - (CONF_KERNEL - 1)
    group = CONV_STRIDE * SUBLANES

    def slab(sl, carry):
        ws = [cw_ref[sl, j:j + 1, :] for j in range(CONF_KERNEL)]
        bias = jnp.broadcast_to(cb_ref[sl], (SUBLANES, LANES))
        for gi in range(tm // group):
            t0 = gi * group
            accs = [bias] * CONV_STRIDE
            for d in range(base, base + CONV_STRIDE - 1 + CONF_KERNEL):
                win = ubuf_sc[sl, pl.ds(t0 + d, SUBLANES, stride=CONV_STRIDE), :]
                for r in range(CONV_STRIDE):
                    j = d - base - r
                    if 0 <= j < CONF_KERNEL:
                        accs[r] = accs[r] + ws[j] * win
            for r in range(CONV_STRIDE):
                conv_sc[sl, pl.ds(t0 + r, SUBLANES, stride=CONV_STRIDE), :] = accs[r]
        return carry

    lax.fori_loop(0, nslab, slab, 0)

    c = jnp.concatenate([conv_sc[sl] for sl in range(nslab)], axis=1)
    mu = jnp.mean(c, axis=-1, keepdims=True)
    cc = c - mu
    var = jnp.mean(cc * cc, axis=-1, keepdims=True)
    ln = cc * lax.rsqrt(var + NORM_EPS) * lnw_ref[...] + lnb_ref[...]
    act = (_silu(ln) * _silu(g_ref[0].astype(F32))).astype(BF16)
    h = h_ref[0] + jnp.dot(act, wo_ref[...], preferred_element_type=F32)
    h = _ple(h, p_ref.at[0], pw_ref, gw_ref)
    out_ref[0] = _rms_rows(h, fw_ref[...])


def _conv_entry(u, g, h, p, conv_w, conv_b, ln_w, ln_b, wo, pw, gw, fw):
    row = lambda v: v.reshape(1, -1).astype(F32)
    halo = CONV_HALO
    nslab = conv_w.shape[1] // LANES
    conv_w = jnp.pad(conv_w.astype(F32), ((0, halo - CONF_KERNEL), (0, 0)))
    conv_w = conv_w.reshape(halo, nslab, LANES).transpose(1, 0, 2)
    conv_b = conv_b.astype(F32).reshape(nslab, 1, LANES)
    return _conv(u, g, h, p, conv_w, conv_b, row(ln_w), row(ln_b), wo, pw, gw, row(fw),
                 tm=_pick(u.shape[1], CONV_TILE), halo=halo)


def _conv(u, g, h, p, conv_w, conv_b, ln_w, ln_b, wo, pw, gw, fw, *, tm, halo):
    b, s, cw = u.shape
    d = h.shape[-1]
    per = tm // halo
    blk = lambda n: pl.BlockSpec((1, tm, n), lambda bi, i: (bi, i, 0))
    halo_spec = pl.BlockSpec((1, halo, cw), lambda bi, i: (bi, jnp.maximum(i * per - 1, 0), 0))
    consts = [conv_w, conv_b, ln_w, ln_b, wo, pw, gw, fw]
    return pl.pallas_call(
        functools.partial(_conv_kernel, tm=tm, halo=halo),
        grid=(b, s // tm),
        in_specs=[blk(cw), halo_spec, blk(cw), blk(d), blk(p.shape[-1])]
                 + [_const_spec(c.shape) for c in consts],
        out_specs=blk(d),
        out_shape=jax.ShapeDtypeStruct((b, s, d), F32),
        scratch_shapes=[pltpu.VMEM((cw // LANES, halo + tm, LANES), F32),
                        pltpu.VMEM((cw // LANES, tm, LANES), F32)],
        compiler_params=_params(("parallel", "parallel")),
        name="conv_tail",
    )(u, u, g, h, p, *consts)


def _rope_tables(seq):
    half = DIFF_HEAD_DIM // 2
    pos = jnp.arange(seq, dtype=F32)
    inv = ROPE_THETA ** (-jnp.arange(0, DIFF_HEAD_DIM, 2, dtype=F32) / DIFF_HEAD_DIM)
    ang = pos[:, None] * inv[None, :]
    cos, sin = jnp.cos(ang), jnp.sin(ang)
    reps = LANES // DIFF_HEAD_DIM
    cos_t = jnp.tile(jnp.concatenate([cos, cos], axis=1), (1, reps))
    sin_t = jnp.tile(jnp.concatenate([-sin, sin], axis=1), (1, reps))
    return cos_t, sin_t


def _pick(n, pref):
    while n % pref:
        pref //= 2
    return pref


def kernel(x, p, norm_w, ple_w, ple_gate_w, even_w_in, ssd_conv_w, ssd_conv_b, ssd_dt_bias, ssd_a_log, ssd_d, ssd_norm_w, diff_lambda, diff_subln_w, even_w_out, conf_w_in, conf_conv_w, conf_conv_b, conf_ln_w, conf_ln_b, conf_w_out, final_norm_w):
    b, s, d = x.shape
    t = b * s
    x2 = x.reshape(t, d)
    row = lambda v: v.reshape(1, -1).astype(F32)

    w_in = even_w_in[0].astype(BF16)
    o_z, o_xbc = SSD_WIDTH, SSD_WIDTH + SSD_CONV_CH
    o_dt = o_xbc + SSD_HEADS
    o_q, o_k, o_v = o_dt + DIFF_WIDTH, o_dt + 2 * DIFF_WIDTH, o_dt + 3 * DIFF_WIDTH
    w_dt = jnp.pad(w_in[:, o_xbc:o_dt], ((0, 0), (0, LANES - SSD_HEADS)))
    ws0 = (w_in[:, :o_z], w_in[:, o_z:o_xbc], w_dt, w_in[:, o_dt:o_q], w_in[:, o_q:o_k],
           w_in[:, o_k:o_v], w_in[:, o_v:])
    cos_t, sin_t = _rope_tables(s)
    z, xbc, dt, q, k, v, g = _proj0(x2, row(norm_w[0]), cos_t, sin_t, ws0, seq=s,
                                    tm=_pick(s, PROJ0_TILE))

    y = _ssd_entry(xbc.reshape(b, s, -1), dt.reshape(b, s, -1), z.reshape(b, s, -1),
                   ssd_conv_w[0], ssd_conv_b[0], ssd_dt_bias[0], ssd_a_log[0], ssd_d[0],
                   ssd_norm_w[0])

    o = _attn_entry(q.reshape(b, s, -1), k.reshape(b, s, -1), v.reshape(b, s, -1),
                    g.reshape(b, s, -1), diff_lambda[0], diff_subln_w[0])

    w_out = even_w_out[0].astype(BF16)
    h1 = _out0(y.reshape(t, -1), o.reshape(t, -1), x2, p[0].reshape(t, -1),
               w_out[:SSD_WIDTH], w_out[SSD_WIDTH:], ple_w[0].astype(BF16),
               ple_gate_w[0].astype(BF16), tm=_pick(t, OUT0_TILE))

    w1 = conf_w_in[0].astype(BF16)
    cw = w1.shape[1] // 3
    u, g1 = _proj1(h1, row(norm_w[1]), w1[:, :cw], w1[:, cw:2 * cw], w1[:, 2 * cw:],
                   tm=_pick(t, PROJ1_TILE))
    return _conv_entry(u.reshape(b, s, cw), g1.reshape(b, s, cw), h1.reshape(b, s, d), p[1],
                       conf_conv_w[0], conf_conv_b[0], conf_ln_w[0], conf_ln_b[0],
                       conf_w_out[0].astype(BF16), ple_w[1].astype(BF16),
                       ple_gate_w[1].astype(BF16), final_norm_w)
```

```python
import functools
import math

import jax
import jax.numpy as jnp
from jax import lax
from jax.experimental import pallas as pl
from jax.experimental.pallas import tpu as pltpu

F32 = jnp.float32
BF16 = jnp.bfloat16

NORM_EPS = 1e-6
ROPE_THETA = 10000.0

SSD_HEADS = 16
SSD_HEAD_DIM = 64
SSD_WIDTH = SSD_HEADS * SSD_HEAD_DIM
SSD_GROUPS = 2
SSD_STATE = 128
SSD_CONV = 4
SSD_CHUNK = 128
SSD_BC = SSD_GROUPS * SSD_STATE
SSD_CONV_CH = SSD_WIDTH + 2 * SSD_BC
SSD_GROUP_WIDTH = SSD_WIDTH // SSD_GROUPS
SSD_CHUNKS_PER_STEP = 2
SSD_TAIL = 16

DIFF_HEADS = 8
DIFF_HEAD_DIM = 64
DIFF_V_DIM = 2 * DIFF_HEAD_DIM
DIFF_WIDTH = DIFF_HEADS * DIFF_V_DIM

CONF_KERNEL = 31

LANES = 128
SUBLANES = 8
VMEM_LIMIT = 48 * 1024 * 1024
MASK_VALUE = -1e30
ATTN_TILE = 1024
ATTN_SUM_ROWS = 16
PROJ0_TILE = 512
PROJ1_TILE = 256
OUT0_TILE = 512
CONV_TILE = 512
CONV_HALO = 32
CONV_STRIDE = 4

HIGHEST = lax.Precision.HIGHEST


def _silu(x):
    return x * jax.nn.sigmoid(x)


def _rms_rows(x, w):
    ms = jnp.mean(x * x, axis=-1, keepdims=True)
    return x * lax.rsqrt(ms + NORM_EPS) * w


def _const_spec(shape):
    return pl.BlockSpec(shape, lambda *_: (0,) * len(shape), pipeline_mode=pl.Buffered(1))


def _params(sem):
    return pltpu.CompilerParams(dimension_semantics=sem, vmem_limit_bytes=VMEM_LIMIT)


def _proj0_kernel(x_ref, nw_ref, cos_ref, sin_ref, wz, wxbc, wdt, wq, wk, wv, wg,
                  z_o, xbc_o, dt_o, q_o, k_o, v_o, g_o, *, q_scale):
    hn = _rms_rows(x_ref[...], nw_ref[...]).astype(BF16)

    def mm(w_ref):
        return jnp.dot(hn, w_ref[...], preferred_element_type=F32)

    z_o[...] = _silu(mm(wz)).astype(BF16)
    xbc_o[...] = mm(wxbc).astype(BF16)
    dt_o[...] = mm(wdt)
    v_o[...] = mm(wv).astype(BF16)
    g_o[...] = _silu(mm(wg)).astype(BF16)

    width = q_o.shape[-1]
    reps = width // LANES
    cos = jnp.tile(cos_ref[...], (1, reps))
    sin = jnp.tile(sin_ref[...], (1, reps))
    lane = lax.broadcasted_iota(jnp.int32, (1, width), 1)
    first_half = (lane % DIFF_HEAD_DIM) < (DIFF_HEAD_DIM // 2)

    def rope(t):
        partner = jnp.where(first_half,
                            pltpu.roll(t, width - DIFF_HEAD_DIM // 2, axis=1),
                            pltpu.roll(t, DIFF_HEAD_DIM // 2, axis=1))
        return t * cos + partner * sin

    q_o[...] = (rope(mm(wq)) * q_scale).astype(BF16)
    k_o[...] = rope(mm(wk)).astype(BF16)


def _proj0(x2, norm_w, cos_t, sin_t, ws, *, seq, tm):
    t, d = x2.shape
    wz, wxbc, wdt, wq, wk, wv, wg = ws
    nblk_seq = seq // tm
    row = lambda n: pl.BlockSpec((tm, n), lambda i: (i, 0))
    pos = pl.BlockSpec((tm, LANES), lambda i: (i % nblk_seq, 0))
    outs = [(SSD_WIDTH, BF16), (SSD_CONV_CH, BF16), (LANES, F32), (DIFF_WIDTH, BF16),
            (DIFF_WIDTH, BF16), (DIFF_WIDTH, BF16), (DIFF_WIDTH, BF16)]
    return pl.pallas_call(
        functools.partial(_proj0_kernel, q_scale=DIFF_HEAD_DIM ** -0.5 * math.log2(math.e)),
        grid=(t // tm,),
        in_specs=[row(d), _const_spec((1, d)), pos, pos] + [_const_spec(w.shape) for w in ws],
        out_specs=[row(n) for n, _ in outs],
        out_shape=[jax.ShapeDtypeStruct((t, n), dt) for n, dt in outs],
        compiler_params=_params(("parallel",)),
        name="proj0",
    )(x2, norm_w, cos_t, sin_t, *ws)


def _ssd_kernel(xbc_ref, halo_ref, dt_ref, z_ref, cw_ref, cb_ref, dtb_ref, aneg_ref,
                dskip_ref, nw_ref, expand_ref, shift_ref, y_ref, state_sc, *, chunks):
    L = SSD_CHUNK
    first = pl.program_id(1) == 0

    @pl.when(first)
    def _():
        state_sc[...] = jnp.zeros_like(state_sc)

    prev_rows = halo_ref[0]
    prev_rows = jnp.where(first, jnp.zeros_like(prev_rows), prev_rows)

    for c in range(chunks):
        lo_row = c * L
        if c == 0:
            win = jnp.concatenate([prev_rows, xbc_ref[0, 0:L, :]], axis=0)
        else:
            win = xbc_ref[0, lo_row - SSD_TAIL:lo_row + L, :]
        y = _ssd_chunk(win, dt_ref[0, lo_row:lo_row + L, :], z_ref[0, lo_row:lo_row + L, :],
                       cw_ref, cb_ref, dtb_ref, aneg_ref, dskip_ref, nw_ref, expand_ref, shift_ref,
                       state_sc)
        y_ref[0, lo_row:lo_row + L, :] = y


def _ssd_chunk(win, dt_raw, z_bf, cw_ref, cb_ref, dtb_ref, aneg_ref, dskip_ref, nw_ref,
               expand_ref, shift_ref, state_sc):
    L = SSD_CHUNK
    taps = jnp.dot(shift_ref[...], win, preferred_element_type=F32)
    acc = cb_ref[...] + cw_ref[0:1, :] * taps[0:L]
    for j in range(1, SSD_CONV):
        acc = acc + cw_ref[j:j + 1, :] * taps[j * L:(j + 1) * L]
    xbc = _silu(acc)
    xs = xbc[:, :SSD_WIDTH]
    bm = xbc[:, SSD_WIDTH:SSD_WIDTH + SSD_BC].astype(BF16)
    cm = xbc[:, SSD_WIDTH + SSD_BC:].astype(BF16)

    dt = jax.nn.softplus(dt_raw + dtb_ref[...])
    a = dt * aneg_ref[...]
    r_i = lax.broadcasted_iota(jnp.int32, (L, L), 0)
    c_i = lax.broadcasted_iota(jnp.int32, (L, L), 1)
    causal = r_i >= c_i
    tril = causal.astype(F32)
    acs = jnp.dot(tril, a, precision=HIGHEST, preferred_element_type=F32)
    acs_t = acs.T
    both = jnp.concatenate([dt, acs], axis=0)
    hi = both.astype(BF16)
    rest = both - hi.astype(F32)
    mid = rest.astype(BF16)
    lo = (rest - mid.astype(F32)).astype(BF16)
    both_x = jnp.dot(jnp.concatenate([hi, mid, lo], axis=1), expand_ref[...],
                     preferred_element_type=F32)
    dt_x = both_x[:L]
    acs_x = both_x[L:]
    acs_last = acs_x[L - 1:L, :]

    xd = xs * dt_x
    xd_bf = xd.astype(BF16)
    xdd_bf = (xd * jnp.exp(acs_last - acs_x)).astype(BF16)
    grow = jnp.exp(acs_x)
    chunk_decay = jnp.exp(acs_last)

    nt = (((1,), (1,)), ((), ()))
    tn = (((0,), (0,)), ((), ()))
    heads_per_group = SSD_HEADS // SSD_GROUPS
    quad_w = 2 * LANES
    heads_per_quad = quad_w // SSD_HEAD_DIM
    lane_q = lax.broadcasted_iota(jnp.int32, (1, quad_w), 1) // SSD_HEAD_DIM

    y_parts = []
    for g in range(SSD_GROUPS):
        bg = bm[:, g * SSD_STATE:(g + 1) * SSD_STATE]
        cg = cm[:, g * SSD_STATE:(g + 1) * SSD_STATE]
        gs = slice(g * SSD_GROUP_WIDTH, (g + 1) * SSD_GROUP_WIDTH)
        cb = lax.dot_general(cg, bg, nt, preferred_element_type=F32)
        prev = state_sc[g]
        y_g = jnp.dot(cg, prev.astype(BF16), preferred_element_type=F32) * grow[:, gs]
        new = lax.dot_general(bg, xdd_bf[:, gs], tn, preferred_element_type=F32)
        state_sc[g] = prev * chunk_decay[:, gs] + new
        quads = []
        for qd in range(SSD_GROUP_WIDTH // quad_w):
            lo = g * SSD_GROUP_WIDTH + qd * quad_w
            x_q = xd_bf[:, lo:lo + quad_w]
            y_q = jnp.zeros((L, quad_w), F32)
            for hq in range(heads_per_quad):
                h = g * heads_per_group + qd * heads_per_quad + hq
                diff = jnp.broadcast_to(acs[:, h:h + 1], (L, L)) - acs_t[h:h + 1, :]
                lmat = jnp.where(causal, jnp.exp(jnp.where(causal, diff, 0.0)), 0.0)
                m = (cb * lmat).astype(BF16)
                r = jnp.dot(m, x_q, preferred_element_type=F32)
                y_q = jnp.where(lane_q == hq, r, y_q)
            quads.append(y_q)
        y_parts.append(y_g + jnp.concatenate(quads, axis=1))

    z = z_bf.astype(F32)
    outs = []
    for g in range(SSD_GROUPS):
        gs = slice(g * SSD_GROUP_WIDTH, (g + 1) * SSD_GROUP_WIDTH)
        yz = (y_parts[g] + xs[:, gs] * dskip_ref[:, gs]) * z[:, gs]
        outs.append(_rms_rows(yz, nw_ref[:, gs]))
    return jnp.concatenate(outs, axis=1).astype(BF16)


def _ssd_entry(xbc, dt, z, conv_w, conv_b, dt_bias, a_log, d_skip, norm_w):
    row = lambda v: v.reshape(1, -1).astype(F32)
    pad_h = lambda vec: jnp.pad(vec.astype(F32), (0, LANES - SSD_HEADS)).reshape(1, LANES)
    head_of_channel = jnp.arange(SSD_WIDTH) // SSD_HEAD_DIM
    expand = (jnp.arange(LANES)[:, None] == head_of_channel[None, :]).astype(BF16)
    a_neg = -jnp.exp(a_log.astype(F32))
    return _ssd(xbc, dt, z, conv_w.astype(F32), row(conv_b), pad_h(dt_bias), pad_h(a_neg),
                row(d_skip[head_of_channel]), row(norm_w), jnp.tile(expand, (3, 1)),
                _shift_matrix(SSD_CONV, SSD_CHUNK, SSD_TAIL))


def _shift_matrix(taps, rows, tail):
    out_row = jnp.arange(taps * rows)
    src = out_row % rows + tail - (taps - 1 - out_row // rows)
    return (src[:, None] == jnp.arange(tail + rows)[None, :]).astype(BF16)


def _ssd(xbc, dt, z, conv_w, conv_b, dt_bias, a_neg, d_skip_x, norm_w, expand, shift):
    b, s, _ = xbc.shape
    chunks = _pick(s // SSD_CHUNK, SSD_CHUNKS_PER_STEP)
    rows = chunks * SSD_CHUNK
    blk = lambda n: pl.BlockSpec((1, rows, n), lambda bi, ci: (bi, ci, 0))
    consts = [conv_w, conv_b, dt_bias, a_neg, d_skip_x, norm_w, expand, shift]
    per = rows // SSD_TAIL
    halo = pl.BlockSpec((1, SSD_TAIL, SSD_CONV_CH),
                        lambda bi, ci: (bi, jnp.maximum(ci * per - 1, 0), 0))
    return pl.pallas_call(
        functools.partial(_ssd_kernel, chunks=chunks),
        grid=(b, s // rows),
        in_specs=[blk(SSD_CONV_CH), halo, blk(LANES), blk(SSD_WIDTH)]
                 + [_const_spec(c.shape) for c in consts],
        out_specs=blk(SSD_WIDTH),
        out_shape=jax.ShapeDtypeStruct((b, s, SSD_WIDTH), BF16),
        scratch_shapes=[pltpu.VMEM((SSD_GROUPS, SSD_STATE, SSD_GROUP_WIDTH), F32)],
        compiler_params=_params(("parallel", "arbitrary")),
        name="ssd",
    )(xbc, xbc, dt, z, *consts)


def _attn_kernel(lam_ref, q_ref, k_ref, v_ref, g_ref, sw_ref, o_ref,
                 qm_sc, sa_sc, sb_sc, mxa_sc, mxb_sc, m_sc, l_sc, acc_sc, *, tq, lambda_init):
    qi = pl.program_id(2)
    nt = (((1,), (1,)), ((), ()))

    q = q_ref[0]
    lane = lax.broadcasted_iota(jnp.int32, q.shape, 1)
    zero = jnp.zeros_like(q)
    qm_sc[0] = jnp.where(lane < DIFF_HEAD_DIM, q, zero)
    qm_sc[1] = jnp.where(lane >= DIFF_HEAD_DIM, q, zero)

    m_sc[...] = jnp.full(m_sc.shape, MASK_VALUE, F32)
    l_sc[...] = jnp.zeros(l_sc.shape, F32)
    acc_sc[...] = jnp.zeros(acc_sc.shape, F32)

    def scores(j, masked, s_ref, mx_ref):
        start = pl.multiple_of(j * tq, tq)
        k = k_ref[0, pl.ds(start, tq), :]
        for c in range(2):
            s = lax.dot_general(k, qm_sc[c], nt, preferred_element_type=F32)
            if masked:
                kpos = lax.broadcasted_iota(jnp.int32, s.shape, 0)
                qpos = lax.broadcasted_iota(jnp.int32, s.shape, 1)
                s = jnp.where(kpos <= qpos, s, MASK_VALUE)
            s_ref[c] = s
            mx_ref[c] = jnp.max(s, axis=0, keepdims=True)

    def consume(j, s_ref, mx_ref):
        start = pl.multiple_of(j * tq, tq)
        vt = jnp.concatenate([v_ref[0, pl.ds(start, tq), :].T,
                              jnp.ones((ATTN_SUM_ROWS, tq), BF16)], axis=0)
        for c in range(2):
            m_old = m_sc[c]
            m_new = jnp.maximum(m_old, mx_ref[c])
            p = jnp.exp2(s_ref[c] - m_new).astype(BF16)
            alpha = jnp.exp2(m_old - m_new)
            pv = jnp.dot(vt, p, preferred_element_type=F32)
            l_sc[c] = alpha * l_sc[c] + pv[DIFF_V_DIM:DIFF_V_DIM + 1]
            acc_sc[c] = alpha * acc_sc[c] + pv[:DIFF_V_DIM]
            m_sc[c] = m_new

    scores(qi, True, sa_sc, mxa_sc)

    def pair(i, carry):
        j = 2 * i
        scores(j, False, sb_sc, mxb_sc)
        consume(jnp.where(i == 0, qi, j - 1), sa_sc, mxa_sc)
        scores(j + 1, False, sa_sc, mxa_sc)
        consume(j, sb_sc, mxb_sc)
        return carry

    npairs = qi // 2
    lax.fori_loop(0, npairs, pair, 0)
    pending = jnp.where(npairs == 0, qi, 2 * npairs - 1)

    @pl.when(qi % 2 == 0)
    def _():
        consume(pending, sa_sc, mxa_sc)

    @pl.when(qi % 2 == 1)
    def _():
        scores(qi - 1, False, sb_sc, mxb_sc)
        consume(pending, sa_sc, mxa_sc)
        consume(qi - 1, sb_sc, mxb_sc)


    lv = lam_ref[...]
    lam = (jnp.exp(jnp.sum(lv[0:1] * lv[1:2], axis=1, keepdims=True))
           - jnp.exp(jnp.sum(lv[2:3] * lv[3:4], axis=1, keepdims=True)) + lambda_init)
    o_t = acc_sc[0] / l_sc[0] - lam * (acc_sc[1] / l_sc[1])
    o = _rms_rows(o_t.T, sw_ref[...]) * (1.0 - lambda_init)
    o_ref[0] = (o * g_ref[0].astype(F32)).astype(BF16)


def _attn_entry(q, k, v, g, lam_vecs, subln_w):
    lambda_init = 0.8 - 0.6 * math.exp(-0.3 * 0)
    return _attn(q, k, v, g, lam_vecs.astype(F32), subln_w.reshape(1, -1).astype(F32),
                 lambda_init=lambda_init, tq=_pick(q.shape[1], ATTN_TILE))


def _attn(q, k, v, g, lam_vecs, subln_w, *, lambda_init, tq):
    b, s, _ = q.shape
    qblk = pl.BlockSpec((1, tq, DIFF_V_DIM), lambda bi, hi, qi: (bi, qi, hi))
    kvblk = pl.BlockSpec((1, s, DIFF_V_DIM), lambda bi, hi, qi: (bi, 0, hi))
    stat = pltpu.VMEM((2, 1, tq), F32)
    sbuf = pltpu.VMEM((2, tq, tq), F32)
    return pl.pallas_call(
        functools.partial(_attn_kernel, tq=tq, lambda_init=lambda_init),
        grid=(b, DIFF_HEADS, s // tq),
        in_specs=[_const_spec(lam_vecs.shape), qblk, kvblk, kvblk, qblk, _const_spec(subln_w.shape)],
        out_specs=qblk,
        out_shape=jax.ShapeDtypeStruct((b, s, DIFF_WIDTH), BF16),
        scratch_shapes=[pltpu.VMEM((2, tq, DIFF_V_DIM), BF16), sbuf, sbuf, stat, stat, stat, stat,
                        pltpu.VMEM((2, DIFF_V_DIM, tq), F32)],
        compiler_params=_params(("parallel", "parallel", "arbitrary")),
        name="diff_attn",
    )(lam_vecs, q, k, v, g, subln_w)


def _ple(h, p_ref, pw_ref, gw_ref):
    e = jnp.dot(p_ref[...].astype(BF16), pw_ref[...], preferred_element_type=F32)
    gate = jax.nn.sigmoid(jnp.dot(h.astype(BF16), gw_ref[...], preferred_element_type=F32))
    return h + e * gate


def _out0_kernel(y_ref, o_ref, x_ref, p_ref, wy_ref, wo_ref, pw_ref, gw_ref, h_ref):
    mix = (jnp.dot(y_ref[...], wy_ref[...], preferred_element_type=F32)
           + jnp.dot(o_ref[...], wo_ref[...], preferred_element_type=F32))
    h_ref[...] = _ple(x_ref[...] + mix, p_ref, pw_ref, gw_ref)


def _out0(y2, o2, x2, p2, wy, wo, pw, gw, *, tm):
    t, d = x2.shape
    row = lambda n: pl.BlockSpec((tm, n), lambda i: (i, 0))
    ws = [wy, wo, pw, gw]
    return pl.pallas_call(
        _out0_kernel,
        grid=(t // tm,),
        in_specs=[row(y2.shape[1]), row(o2.shape[1]), row(d), row(p2.shape[1])]
                 + [_const_spec(w.shape) for w in ws],
        out_specs=row(d),
        out_shape=jax.ShapeDtypeStruct((t, d), F32),
        compiler_params=_params(("parallel",)),
        name="out0",
    )(y2, o2, x2, p2, *ws)


def _proj1_kernel(h_ref, nw_ref, wu, wug, wg, u_o, g_o):
    hn = _rms_rows(h_ref[...], nw_ref[...]).astype(BF16)
    u = jnp.dot(hn, wu[...], preferred_element_type=F32)
    ug = jnp.dot(hn, wug[...], preferred_element_type=F32)
    u_o[...] = (u * jax.nn.sigmoid(ug)).astype(BF16)
    g_o[...] = _silu(jnp.dot(hn, wg[...], preferred_element_type=F32)).astype(BF16)


def _proj1(h2, norm_w, wu, wug, wg, *, tm):
    t, d = h2.shape
    cw = wu.shape[1]
    row = lambda n: pl.BlockSpec((tm, n), lambda i: (i, 0))
    ws = [wu, wug, wg]
    return pl.pallas_call(
        _proj1_kernel,
        grid=(t // tm,),
        in_specs=[row(d), _const_spec((1, d))] + [_const_spec(w.shape) for w in ws],
        out_specs=[row(cw), row(cw)],
        out_shape=[jax.ShapeDtypeStruct((t, cw), BF16)] * 2,
        compiler_params=_params(("parallel",)),
        name="proj1",
    )(h2, norm_w, *ws)


def _conv_kernel(u_ref, halo_ref, g_ref, h_ref, p_ref, cw_ref, cb_ref, lnw_ref, lnb_ref,
                 wo_ref, pw_ref, gw_ref, fw_ref, out_ref, ubuf_sc, conv_sc, *, tm, halo):
    width = u_ref.shape[-1]
    nslab = width // LANES
    first = pl.program_id(1) == 0
    halo_rows = halo_ref[0].astype(F32)
    halo_rows = jnp.where(first, jnp.zeros_like(halo_rows), halo_rows)
    u_rows = u_ref[0].astype(F32)
    for sl in range(nslab):
        ubuf_sc[sl, 0:halo, :] = halo_rows[:, sl * LANES:(sl + 1) * LANES]
        ubuf_sc[sl, halo:halo + tm, :] = u_rows[:, sl * LANES:(sl + 1) * LANES]

    base = halo - (CONF_KERNEL - 1)
    group = CONV_STRIDE * SUBLANES

    def slab(sl, carry):
        ws = [cw_ref[sl, j:j + 1, :] for j in range(CONF_KERNEL)]
        bias = jnp.broadcast_to(cb_ref[sl], (SUBLANES, LANES))
        for gi in range(tm // group):
            t0 = gi * group
            accs = [bias] * CONV_STRIDE
            for d in range(base, base + CONV_STRIDE - 1 + CONF_KERNEL):
                win = ubuf_sc[sl, pl.ds(t0 + d, SUBLANES, stride=CONV_STRIDE), :]
                for r in range(CONV_STRIDE):
                    j = d - base - r
                    if 0 <= j < CONF_KERNEL:
                        accs[r] = accs[r] + ws[j] * win
            for r in range(CONV_STRIDE):
                conv_sc[sl, pl.ds(t0 + r, SUBLANES, stride=CONV_STRIDE), :] = accs[r]
        return carry

    lax.fori_loop(0, nslab, slab, 0)

    c = jnp.concatenate([conv_sc[sl] for sl in range(nslab)], axis=1)
    mu = jnp.mean(c, axis=-1, keepdims=True)
    cc = c - mu
    var = jnp.mean(cc * cc, axis=-1, keepdims=True)
    ln = cc * lax.rsqrt(var + NORM_EPS) * lnw_ref[...] + lnb_ref[...]
    act = (_silu(ln) * g_ref[0].astype(F32)).astype(BF16)
    h = h_ref[0] + jnp.dot(act, wo_ref[...], preferred_element_type=F32)
    h = _ple(h, p_ref.at[0], pw_ref, gw_ref)
    out_ref[0] = _rms_rows(h, fw_ref[...])


def _conv_entry(u, g, h, p, conv_w, conv_b, ln_w, ln_b, wo, pw, gw, fw):
    row = lambda v: v.reshape(1, -1).astype(F32)
    halo = CONV_HALO
    nslab = conv_w.shape[1] // LANES
    conv_w = jnp.pad(conv_w.astype(F32), ((0, halo - CONF_KERNEL), (0, 0)))
    conv_w = conv_w.reshape(halo, nslab, LANES).transpose(1, 0, 2)
    conv_b = conv_b.astype(F32).reshape(nslab, 1, LANES)
    return _conv(u, g, h, p, conv_w, conv_b, row(ln_w), row(ln_b), wo, pw, gw, row(fw),
                 tm=_pick(u.shape[1], CONV_TILE), halo=halo)


def _conv(u, g, h, p, conv_w, conv_b, ln_w, ln_b, wo, pw, gw, fw, *, tm, halo):
    b, s, cw = u.shape
    d = h.shape[-1]
    per = tm // halo
    blk = lambda n: pl.BlockSpec((1, tm, n), lambda bi, i: (bi, i, 0))
    halo_spec = pl.BlockSpec((1, halo, cw), lambda bi, i: (bi, jnp.maximum(i * per - 1, 0), 0))
    consts = [conv_w, conv_b, ln_w, ln_b, wo, pw, gw, fw]
    return pl.pallas_call(
        functools.partial(_conv_kernel, tm=tm, halo=halo),
        grid=(b, s // tm),
        in_specs=[blk(cw), halo_spec, blk(cw), blk(d), blk(p.shape[-1])]
                 + [_const_spec(c.shape) for c in consts],
        out_specs=blk(d),
        out_shape=jax.ShapeDtypeStruct((b, s, d), F32),
        scratch_shapes=[pltpu.VMEM((cw // LANES, halo + tm, LANES), F32),
                        pltpu.VMEM((cw // LANES, tm, LANES), F32)],
        compiler_params=_params(("parallel", "parallel")),
        name="conv_tail",
    )(u, u, g, h, p, *consts)


def _rope_tables(seq):
    half = DIFF_HEAD_DIM // 2
    pos = jnp.arange(seq, dtype=F32)
    inv = ROPE_THETA ** (-jnp.arange(0, DIFF_HEAD_DIM, 2, dtype=F32) / DIFF_HEAD_DIM)
    ang = pos[:, None] * inv[None, :]
    cos, sin = jnp.cos(ang), jnp.sin(ang)
    reps = LANES // DIFF_HEAD_DIM
    cos_t = jnp.tile(jnp.concatenate([cos, cos], axis=1), (1, reps))
    sin_t = jnp.tile(jnp.concatenate([-sin, sin], axis=1), (1, reps))
    return cos_t, sin_t


def _pick(n, pref):
    while n % pref:
        pref //= 2
    return pref


def kernel(x, p, norm_w, ple_w, ple_gate_w, even_w_in, ssd_conv_w, ssd_conv_b, ssd_dt_bias, ssd_a_log, ssd_d, ssd_norm_w, diff_lambda, diff_subln_w, even_w_out, conf_w_in, conf_conv_w, conf_conv_b, conf_ln_w, conf_ln_b, conf_w_out, final_norm_w):
    b, s, d = x.shape
    t = b * s
    x2 = x.reshape(t, d)
    row = lambda v: v.reshape(1, -1).astype(F32)

    w_in = even_w_in[0].astype(BF16)
    o_z, o_xbc = SSD_WIDTH, SSD_WIDTH + SSD_CONV_CH
    o_dt = o_xbc + SSD_HEADS
    o_q, o_k, o_v = o_dt + DIFF_WIDTH, o_dt + 2 * DIFF_WIDTH, o_dt + 3 * DIFF_WIDTH
    w_dt = jnp.pad(w_in[:, o_xbc:o_dt], ((0, 0), (0, LANES - SSD_HEADS)))
    ws0 = (w_in[:, :o_z], w_in[:, o_z:o_xbc], w_dt, w_in[:, o_dt:o_q], w_in[:, o_q:o_k],
           w_in[:, o_k:o_v], w_in[:, o_v:])
    cos_t, sin_t = _rope_tables(s)
    z, xbc, dt, q, k, v, g = _proj0(x2, row(norm_w[0]), cos_t, sin_t, ws0, seq=s,
                                    tm=_pick(s, PROJ0_TILE))

    y = _ssd_entry(xbc.reshape(b, s, -1), dt.reshape(b, s, -1), z.reshape(b, s, -1),
                   ssd_conv_w[0], ssd_conv_b[0], ssd_dt_bias[0], ssd_a_log[0], ssd_d[0],
                   ssd_norm_w[0])

    o = _attn_entry(q.reshape(b, s, -1), k.reshape(b, s, -1), v.reshape(b, s, -1),
                    g.reshape(b, s, -1), diff_lambda[0], diff_subln_w[0])

    w_out = even_w_out[0].astype(BF16)
    h1 = _out0(y.reshape(t, -1), o.reshape(t, -1), x2, p[0].reshape(t, -1),
               w_out[:SSD_WIDTH], w_out[SSD_WIDTH:], ple_w[0].astype(BF16),
               ple_gate_w[0].astype(BF16), tm=_pick(t, OUT0_TILE))

    w1 = conf_w_in[0].astype(BF16)
    cw = w1.shape[1] // 3
    u, g1 = _proj1(h1, row(norm_w[1]), w1[:, :cw], w1[:, cw:2 * cw], w1[:, 2 * cw:],
                   tm=_pick(t, PROJ1_TILE))
    return _conv_entry(u.reshape(b, s, cw), g1.reshape(b, s, cw), h1.reshape(b, s, d), p[1],
                       conf_conv_w[0], conf_conv_b[0], conf_ln_w[0], conf_ln_b[0],
                       conf_w_out[0].astype(BF16), ple_w[1].astype(BF16),
                       ple_gate_w[1].astype(BF16), final_norm_w)
```
